```python
import math
import jax, jax.numpy as jnp
from jax import lax
import numpy as np

D_MODEL = 1024
BATCH = 1
SEQ = 16384
DEPTH = 2
DEC_BATCH = 32
DEC_SEQ = 8
PAST_LEN = 16384
PAGE_SIZE = 128

N_GROUPS = 4
HEADS_PER_GROUP = 4
N_HEADS = N_GROUPS * HEADS_PER_GROUP
HEAD_DIM = D_MODEL // N_HEADS
MIX_WIDTH = N_HEADS * HEAD_DIM
DIFF_QK_DIM = HEAD_DIM // 2
N_PROJ = 3 * MIX_WIDTH + HEADS_PER_GROUP
MOBA_BLOCK = 256
MOBA_TOPK = 3
Q_BLOCK = 128
D_FF = 128 * ((8 * D_MODEL // 3 + 127) // 128)
N_NORMS = 6
FORGET_BIAS_INIT = 4.0
RMS_EPS = 1e-6
NEG_INF = -1e30
G_MOBA, G_FOX, G_SB, G_DIFF = 0, 1, 2, 3

kernel_name = "hybrid_moba_fox_stickbreak_diff_step"


def alibi_slopes():
    n = 2 * HEADS_PER_GROUP
    s = np.array([2.0 ** (-8.0 * (i + 1) / n) for i in range(n)], np.float32)
    return jnp.asarray(s[0::2]), jnp.asarray(s[1::2])


def rms_norm(x, g):
    xf = x.astype(jnp.float32)
    y = xf * lax.rsqrt(jnp.mean(xf * xf, axis=-1, keepdims=True) + RMS_EPS)
    return (y * g.astype(jnp.float32)).astype(x.dtype)


def swiglu(x, w_gate, w_up, w_down):
    return (jax.nn.silu(x @ w_gate) * (x @ w_up)) @ w_down


def group(a, g):
    return a[:, :, g * HEADS_PER_GROUP:(g + 1) * HEADS_PER_GROUP]


def project(hn, w_in, b_f):
    B, T, _ = hn.shape
    z = (hn @ w_in).astype(jnp.float32)
    qkv = z[..., :3 * MIX_WIDTH].reshape(B, T, 3, N_HEADS, HEAD_DIM)
    logf = jax.nn.log_sigmoid(z[..., 3 * MIX_WIDTH:] + b_f.astype(jnp.float32))
    return qkv[:, :, 0], qkv[:, :, 1], qkv[:, :, 2], logf


def sweep_query_blocks(fn, *qarrs):
    B, T = qarrs[0].shape[:2]
    nqb = T // Q_BLOCK
    blocks = tuple(jnp.moveaxis(a.reshape((B, nqb, Q_BLOCK) + a.shape[2:]), 1, 0) for a in qarrs)
    out = lax.map(lambda xs: fn(*xs), (jnp.arange(nqb),) + blocks)
    return jnp.moveaxis(out, 0, 1).reshape((B, T) + out.shape[3:])


def qpos_of(j):
    return j * Q_BLOCK + jnp.arange(Q_BLOCK)


def fox_core(q, k, v, cq, ck, qpos, kpos):
    s = jnp.einsum('bthd,bshd->bhts', q, k) * (HEAD_DIM ** -0.5)
    s = s + jnp.swapaxes(cq, 1, 2)[..., :, None] - jnp.swapaxes(ck, 1, 2)[..., None, :]
    s = jnp.where(kpos[None, :] <= qpos[:, None], s, NEG_INF)
    return jnp.einsum('bhts,bshd->bthd', jax.nn.softmax(s, axis=-1), v)


def stick_core(q, k, v, qpos, kpos):
    z = jnp.einsum('bthd,bshd->bhts', q, k) * (HEAD_DIM ** -0.5)
    mask = kpos[None, :] < qpos[:, None]
    log_keep = jnp.where(mask, jax.nn.log_sigmoid(-z), 0.0)
    rest = lax.cumsum(log_keep, axis=z.ndim - 1, reverse=True) - log_keep
    a = jnp.where(mask, jnp.exp(jax.nn.log_sigmoid(z) + rest), 0.0)
    return jnp.einsum('bhts,bshd->bthd', a, v)


def diff_core(q, k, v, qpos, kpos, slopes, lam, subln_g, lam_init):
    scale = DIFF_QK_DIM ** -0.5
    dist = (qpos[:, None] - kpos[None, :]).astype(jnp.float32)
    bias = -slopes[:, None, None] * dist[None]
    mask = kpos[None, :] <= qpos[:, None]

    def attn_map(qa, ka):
        s = jnp.einsum('bthd,bshd->bhts', qa, ka) * scale + bias
        return jax.nn.softmax(jnp.where(mask, s, NEG_INF), axis=-1)

    p = attn_map(q[..., :DIFF_QK_DIM], k[..., :DIFF_QK_DIM]) - lam * attn_map(q[..., DIFF_QK_DIM:], k[..., DIFF_QK_DIM:])
    o = jnp.einsum('bhts,bshd->bthd', p, v)
    o = o * lax.rsqrt(jnp.mean(o * o, axis=-1, keepdims=True) + RMS_EPS) * subln_g.astype(jnp.float32)
    return o * (1.0 - lam_init)


def moba_select(q, kmean, qpos, topk):
    g = jnp.einsum('bthd,bnhd->bthn', q, kmean)
    qblk = qpos // MOBA_BLOCK
    ok = jnp.arange(kmean.shape[1])[None, :] < qblk[:, None]
    g = jnp.where(ok[None, :, None, :], g, NEG_INF)
    _, idx = lax.top_k(g, topk)
    valid = idx < qblk[None, :, None, None]
    return idx, valid


def moba_gather(kbl, vbl, idx, valid):
    B, T, H, K = idx.shape
    bi = jnp.arange(B)[:, None, None, None]
    hi = jnp.arange(H)[None, None, :, None]
    sk = kbl[bi, idx, :, hi].reshape(B, T, H, K * MOBA_BLOCK, -1)
    sv = vbl[bi, idx, :, hi].reshape(B, T, H, K * MOBA_BLOCK, -1)
    pos = (idx[..., None] * MOBA_BLOCK + jnp.arange(MOBA_BLOCK)).reshape(B, T, H, K * MOBA_BLOCK)
    val = jnp.broadcast_to(valid[..., None], (B, T, H, K, MOBA_BLOCK)).reshape(B, T, H, K * MOBA_BLOCK)
    return sk, sv, pos, val


def moba_core(q, qpos, own_k, own_v, own_pos, slopes, sel):
    scale = HEAD_DIM ** -0.5
    sl = slopes[None, None, :, None]
    s_own = jnp.einsum('bthd,bshd->bths', q, own_k) * scale - sl * (qpos[:, None] - own_pos[None, :]).astype(jnp.float32)[None, :, None, :]
    s_own = jnp.where((own_pos[None, :] <= qpos[:, None])[None, :, None, :], s_own, NEG_INF)
    if sel is None:
        return jnp.einsum('bths,bshd->bthd', jax.nn.softmax(s_own, axis=-1), own_v)
    sel_k, sel_v, sel_pos, sel_valid = sel
    s_sel = jnp.einsum('bthd,bthkd->bthk', q, sel_k) * scale - sl * (qpos[None, :, None, None] - sel_pos).astype(jnp.float32)
    s_sel = jnp.where(sel_valid, s_sel, NEG_INF)
    p = jax.nn.softmax(jnp.concatenate([s_sel, s_own], axis=-1), axis=-1)
    n = s_sel.shape[-1]
    return jnp.einsum('bthk,bthkd->bthd', p[..., :n], sel_v) + jnp.einsum('bths,bshd->bthd', p[..., n:], own_v)


def moba_prompt(q, k, v, slopes):
    B, S, H, d = q.shape
    nb = -(-S // MOBA_BLOCK)
    pad = ((0, 0), (0, nb * MOBA_BLOCK - S), (0, 0), (0, 0))
    kbl = jnp.pad(k, pad).reshape(B, nb, MOBA_BLOCK, H, d)
    vbl = jnp.pad(v, pad).reshape(B, nb, MOBA_BLOCK, H, d)
    n_cand = (S - 1) // MOBA_BLOCK
    topk = min(MOBA_TOPK, n_cand)
    if topk > 0:
        kmean = jnp.mean(kbl[:, :n_cand], axis=2)
        idx, valid = moba_select(q, kmean, jnp.arange(S), topk)
        qarrs = (q, idx, valid)
    else:
        qarrs = (q,)

    def blk(j, qb, *sel_in):
        ob = (j * Q_BLOCK) // MOBA_BLOCK
        own_k = lax.dynamic_index_in_dim(kbl, ob, axis=1, keepdims=False)
        own_v = lax.dynamic_index_in_dim(vbl, ob, axis=1, keepdims=False)
        own_pos = ob * MOBA_BLOCK + jnp.arange(MOBA_BLOCK)
        sel = moba_gather(kbl, vbl, *sel_in) if sel_in else None
        return moba_core(qb, qpos_of(j), own_k, own_v, own_pos, slopes, sel)

    return sweep_query_blocks(blk, *qarrs)


def moba_sample(q, k_new, v_new, k_past, v_past, slopes):
    B, T, H, d = q.shape
    P = k_past.shape[1]
    ob = P // MOBA_BLOCK
    qpos = P + jnp.arange(T)
    own_k = jnp.concatenate([k_past[:, ob * MOBA_BLOCK:], k_new], axis=1)
    own_v = jnp.concatenate([v_past[:, ob * MOBA_BLOCK:], v_new], axis=1)
    own_pos = jnp.arange(ob * MOBA_BLOCK, P + T)
    topk = min(MOBA_TOPK, ob)
    sel = None
    if topk > 0:
        kbl = k_past[:, :ob * MOBA_BLOCK].reshape(B, ob, MOBA_BLOCK, H, d)
        vbl = v_past[:, :ob * MOBA_BLOCK].reshape(B, ob, MOBA_BLOCK, H, d)
        kmean = jnp.mean(kbl, axis=2)
        idx, valid = moba_select(q, kmean, qpos, topk)
        sel = moba_gather(kbl, vbl, idx, valid)
    return moba_core(q, qpos, own_k, own_v, own_pos, slopes, sel)


def mixer_prompt(hn, w_in, b_f, w_out, lam, subln_g, lam_init):
    B, S, _ = hn.shape
    q, k, v, logf = project(hn, w_in, b_f)
    moba_sl, diff_sl = alibi_slopes()
    kpos = jnp.arange(S)
    o_moba = moba_prompt(group(q, G_MOBA), group(k, G_MOBA), group(v, G_MOBA), moba_sl)
    c = jnp.cumsum(logf, axis=1)
    kf, vf = group(k, G_FOX), group(v, G_FOX)
    o_fox = sweep_query_blocks(lambda j, qb, cb: fox_core(qb, kf, vf, cb, c, qpos_of(j), kpos), group(q, G_FOX), c)
    ks, vs = group(k, G_SB), group(v, G_SB)
    o_sb = sweep_query_blocks(lambda j, qb: stick_core(qb, ks, vs, qpos_of(j), kpos), group(q, G_SB))
    kd, vd = group(k, G_DIFF), group(v, G_DIFF)
    o_diff = sweep_query_blocks(lambda j, qb: diff_core(qb, kd, vd, qpos_of(j), kpos, diff_sl, lam, subln_g, lam_init), group(q, G_DIFF))
    o = jnp.concatenate([o_moba, o_fox, o_sb, o_diff], axis=2).reshape(B, S, MIX_WIDTH)
    y = o.astype(hn.dtype) @ w_out
    return y, (k.astype(hn.dtype), v.astype(hn.dtype), logf)


def gather_pages(cache, l, page_table, g):
    rows = cache[l, page_table, :, g * HEADS_PER_GROUP:(g + 1) * HEADS_PER_GROUP]
    B, n_pages, ps = rows.shape[:3]
    return rows.reshape((B, n_pages * ps) + rows.shape[3:]).astype(jnp.float32)


def mixer_sample(hn, l, cache_k, cache_v, cache_logf, page_table, w_in, b_f, w_out, lam, subln_g, lam_init):
    B, T, _ = hn.shape
    q, k, v, logf = project(hn, w_in, b_f)
    moba_sl, diff_sl = alibi_slopes()
    P = page_table.shape[1] * cache_k.shape[2]
    qpos = P + jnp.arange(T)
    kpos = jnp.arange(P + T)

    def past_and_new(g):
        kp = gather_pages(cache_k, l, page_table, g)
        vp = gather_pages(cache_v, l, page_table, g)
        return jnp.concatenate([kp, group(k, g)], axis=1), jnp.concatenate([vp, group(v, g)], axis=1)

    kp = gather_pages(cache_k, l, page_table, G_MOBA)
    vp = gather_pages(cache_v, l, page_table, G_MOBA)
    o_moba = moba_sample(group(q, G_MOBA), group(k, G_MOBA), group(v, G_MOBA), kp, vp, moba_sl)

    kf, vf = past_and_new(G_FOX)
    lf_past = cache_logf[l, page_table].reshape(B, P, HEADS_PER_GROUP).astype(jnp.float32)
    c_past = jnp.cumsum(lf_past, axis=1)
    c_new = c_past[:, -1:] + jnp.cumsum(logf, axis=1)
    o_fox = fox_core(group(q, G_FOX), kf, vf, c_new, jnp.concatenate([c_past, c_new], axis=1), qpos, kpos)

    ks, vs = past_and_new(G_SB)
    o_sb = stick_core(group(q, G_SB), ks, vs, qpos, kpos)

    kd, vd = past_and_new(G_DIFF)
    o_diff = diff_core(group(q, G_DIFF), kd, vd, qpos, kpos, diff_sl, lam, subln_g, lam_init)

    o = jnp.concatenate([o_moba, o_fox, o_sb, o_diff], axis=2).reshape(B, T, MIX_WIDTH)
    y = o.astype(hn.dtype) @ w_out
    return y, (k.astype(hn.dtype), v.astype(hn.dtype), logf)


def trunk_layer(x, g, w_gate, w_up, w_down, mixer):
    h = x + 0.5 * rms_norm(swiglu(rms_norm(x, g[0]), w_gate[0], w_up[0], w_down[0]), g[1])
    m, rows = mixer(rms_norm(h, g[2]))
    h = h + rms_norm(m, g[3])
    h = h + 0.5 * rms_norm(swiglu(rms_norm(h, g[4]), w_gate[1], w_up[1], w_down[1]), g[5])
    return h, rows


def setup_inputs(seed: int = 0) -> dict:
    key = jax.random.key(seed)
    ks = jax.random.split(key, 20)
    n_pages = PAST_LEN // PAGE_SIZE
    n_used = DEC_BATCH * n_pages
    n_phys = n_used + max(1, n_used // 4)

    def nrm(k, shape, scale):
        return scale * jax.random.normal(k, shape, jnp.float32)

    x_prompt = nrm(ks[0], (BATCH, SEQ, D_MODEL), 1.0)
    x_sample = nrm(ks[1], (DEC_BATCH, DEC_SEQ, D_MODEL), 1.0)
    cache_k = nrm(ks[2], (DEPTH, n_phys, PAGE_SIZE, N_HEADS, HEAD_DIM), 1.0)
    cache_v = nrm(ks[3], (DEPTH, n_phys, PAGE_SIZE, N_HEADS, HEAD_DIM), 1.0)
    cache_logf = jax.nn.log_sigmoid(FORGET_BIAS_INIT + nrm(ks[4], (DEPTH, n_phys, PAGE_SIZE, HEADS_PER_GROUP), 1.0))
    page_table = jax.random.permutation(ks[5], n_phys)[:n_used].reshape(DEC_BATCH, n_pages).astype(jnp.int32)
    norm_g = 1.0 + nrm(ks[6], (DEPTH, N_NORMS, D_MODEL), 0.05)
    w_in = nrm(ks[7], (DEPTH, D_MODEL, N_PROJ), D_MODEL ** -0.5)
    b_f = FORGET_BIAS_INIT + nrm(ks[8], (DEPTH, HEADS_PER_GROUP), 0.5)
    w_out = nrm(ks[9], (DEPTH, MIX_WIDTH, D_MODEL), MIX_WIDTH ** -0.5)
    diff_lq1 = nrm(ks[10], (DEPTH, DIFF_QK_DIM), 0.1)
    diff_lk1 = nrm(ks[11], (DEPTH, DIFF_QK_DIM), 0.1)
    diff_lq2 = nrm(ks[12], (DEPTH, DIFF_QK_DIM), 0.1)
    diff_lk2 = nrm(ks[13], (DEPTH, DIFF_QK_DIM), 0.1)
    diff_subln_g = 1.0 + nrm(ks[14], (DEPTH, HEAD_DIM), 0.05)
    ffn_w_gate = nrm(ks[15], (DEPTH, 2, D_MODEL, D_FF), D_MODEL ** -0.5)
    ffn_w_up = nrm(ks[16], (DEPTH, 2, D_MODEL, D_FF), D_MODEL ** -0.5)
    ffn_w_down = nrm(ks[17], (DEPTH, 2, D_FF, D_MODEL), D_FF ** -0.5)
    return {"x_prompt": x_prompt, "x_sample": x_sample, "cache_k": cache_k, "cache_v": cache_v,
            "cache_logf": cache_logf, "page_table": page_table, "norm_g": norm_g, "w_in": w_in,
            "b_f": b_f, "w_out": w_out, "diff_lq1": diff_lq1, "diff_lk1": diff_lk1,
            "diff_lq2": diff_lq2, "diff_lk2": diff_lk2, "diff_subln_g": diff_subln_g,
            "ffn_w_gate": ffn_w_gate, "ffn_w_up": ffn_w_up, "ffn_w_down": ffn_w_down}


def reference(x_prompt, x_sample, cache_k, cache_v, cache_logf, page_table, norm_g, w_in, b_f, w_out,
              diff_lq1, diff_lk1, diff_lq2, diff_lk2, diff_subln_g, ffn_w_gate, ffn_w_up, ffn_w_down):
    hp, hs = x_prompt, x_sample
    pk, pv, pf, sk, sv, sf = [], [], [], [], [], []
    for l in range(DEPTH):
        lam_init = 0.8 - 0.6 * math.exp(-0.3 * l)
        lam = (jnp.exp(jnp.sum(diff_lq1[l].astype(jnp.float32) * diff_lk1[l].astype(jnp.float32)))
               - jnp.exp(jnp.sum(diff_lq2[l].astype(jnp.float32) * diff_lk2[l].astype(jnp.float32))) + lam_init)
        hp, (k_, v_, f_) = trunk_layer(
            hp, norm_g[l], ffn_w_gate[l], ffn_w_up[l], ffn_w_down[l],
            lambda hn: mixer_prompt(hn, w_in[l], b_f[l], w_out[l], lam, diff_subln_g[l], lam_init))
        pk.append(k_)
        pv.append(v_)
        pf.append(f_)
        hs, (k_, v_, f_) = trunk_layer(
            hs, norm_g[l], ffn_w_gate[l], ffn_w_up[l], ffn_w_down[l],
            lambda hn: mixer_sample(hn, l, cache_k, cache_v, cache_logf, page_table,
                                    w_in[l], b_f[l], w_out[l], lam, diff_subln_g[l], lam_init))
        sk.append(k_)
        sv.append(v_)
        sf.append(f_)
    return (hp, hs, jnp.stack(pk), jnp.stack(pv), jnp.stack(pf), jnp.stack(sk), jnp.stack(sv), jnp.stack(sf))
```

```python
import functools
import math

import numpy as np
import jax
import jax.numpy as jnp
from jax import lax
from jax.experimental import pallas as pl
from jax.experimental.pallas import tpu as pltpu

F32 = jnp.float32
BF16 = jnp.bfloat16

D_MODEL = 1024
N_HEADS = 16
HEAD_DIM = 64
GROUP_HEADS = 4
GROUP_WIDTH = GROUP_HEADS * HEAD_DIM
PAIR_WIDTH = 2 * HEAD_DIM
DIFF_QK_DIM = HEAD_DIM // 2
MOBA_BLOCK = 256
MOBA_TOPK = 3
RMS_EPS = 1e-6
NEG_INF = -1e30
LANES = 128
G_MOBA, G_FOX, G_SB, G_DIFF = 0, 1, 2, 3
MOBA_SLOPES = tuple(2.0 ** -(2 * h + 1) for h in range(GROUP_HEADS))
DIFF_SLOPES = tuple(2.0 ** -(2 * h + 2) for h in range(GROUP_HEADS))
VMEM_LIMIT = 56 * 1024 * 1024


def _cparams(n_axes):
    return pltpu.CompilerParams(dimension_semantics=("arbitrary",) * n_axes,
                                vmem_limit_bytes=VMEM_LIMIT)


def _rms(x, g):
    return x * lax.rsqrt(jnp.mean(x * x, axis=-1, keepdims=True) + RMS_EPS) * g


def _dot(a, b):
    return jnp.dot(a, b, preferred_element_type=F32)


def _dot_nt(a, b):
    return lax.dot_general(a, b, (((1,), (1,)), ((), ())), preferred_element_type=F32)


def _split3(a):
    a1 = a.astype(BF16)
    r1 = a - a1.astype(F32)
    a2 = r1.astype(BF16)
    a3 = (r1 - a2.astype(F32)).astype(BF16)
    return a1, a2, a3


def _dot_f32_lhs(a, b01):
    a1, a2, a3 = _split3(a)
    return _dot(a1, b01) + _dot(a2, b01) + _dot(a3, b01)


def _dot_f32_rhs(a01, b):
    b1, b2, b3 = _split3(b)
    return _dot(a01, b1) + _dot(a01, b2) + _dot(a01, b3)


def _log_sigmoid(x):
    return jnp.minimum(x, 0.0) - jnp.log1p(jnp.exp(-jnp.abs(x)))


def _softplus(x):
    return jnp.maximum(x, 0.0) + jnp.log1p(jnp.exp(-jnp.abs(x)))


def _lane_iota(shape):
    return lax.broadcasted_iota(jnp.int32, shape, len(shape) - 1)


def _row_iota(shape):
    return lax.broadcasted_iota(jnp.int32, shape, len(shape) - 2)


def _select_by_index(idx, values):
    out = jnp.full(jnp.shape(idx), values[-1], F32)
    for i in range(len(values) - 2, -1, -1):
        out = jnp.where(idx == i, values[i], out)
    return out


def _ffn_kernel(x_ref, gpre_ref, gpost_ref, wg_ref, wu_ref, wd_ref, o_ref, xn_ref, acc_ref):
    j = pl.program_id(1)

    @pl.when(j == 0)
    def _():
        xn_ref[...] = _rms(x_ref[...], gpre_ref[...]).astype(BF16)
        acc_ref[...] = jnp.zeros_like(acc_ref)

    xn = xn_ref[...]
    gate = _dot(xn, wg_ref[...])
    up = _dot(xn, wu_ref[...])
    mid = gate * jax.nn.sigmoid(gate) * up
    acc_ref[...] += _dot(mid.astype(BF16), wd_ref[...])

    @pl.when(j == pl.num_programs(1) - 1)
    def _():
        o_ref[...] = x_ref[...] + 0.5 * _rms(acc_ref[...], gpost_ref[...])


def _ffn_tiles(rows, d_ff):
    tm = min(rows, 512)
    tf = d_ff
    for cand in (1408, 1024, 512, 256, 128):
        if d_ff % cand == 0:
            tf = cand
            break
    return tm, tf


def ffn_block(x, g_pre, g_post, w_gate, w_up, w_down):
    rows, d = x.shape
    d_ff = w_gate.shape[1]
    tm, tf = _ffn_tiles(rows, d_ff)
    return pl.pallas_call(
        _ffn_kernel,
        grid=(rows // tm, d_ff // tf),
        in_specs=[
            pl.BlockSpec((tm, d), lambda i, j: (i, 0)),
            pl.BlockSpec((1, d), lambda i, j: (0, 0)),
            pl.BlockSpec((1, d), lambda i, j: (0, 0)),
            pl.BlockSpec((d, tf), lambda i, j: (0, j)),
            pl.BlockSpec((d, tf), lambda i, j: (0, j)),
            pl.BlockSpec((tf, d), lambda i, j: (j, 0)),
        ],
        out_specs=pl.BlockSpec((tm, d), lambda i, j: (i, 0)),
        out_shape=jax.ShapeDtypeStruct((rows, d), F32),
        scratch_shapes=[pltpu.VMEM((tm, d), BF16), pltpu.VMEM((tm, d), F32)],
        compiler_params=_cparams(2),
        name="ffn_block",
    )(x, g_pre, g_post, w_gate, w_up, w_down)


def _proj_kernel(with_kmean, h_ref, g_ref, w_ref, wf_ref, bf_ref, *out_refs):
    if with_kmean:
        q_ref, kf_ref, vf_ref, kb_ref, vb_ref, lf_ref, km_ref = out_refs
    else:
        q_ref, kf_ref, vf_ref, kb_ref, vb_ref, lf_ref = out_refs
    hn = _rms(h_ref[...], g_ref[...]).astype(BF16)
    q = _dot(hn, w_ref[:, 0:D_MODEL])
    q_ref[...] = q.astype(q_ref.dtype)
    k = _dot(hn, w_ref[:, D_MODEL:2 * D_MODEL])
    kf_ref[...] = k
    kb_ref[...] = k.astype(BF16)
    v = _dot(hn, w_ref[:, 2 * D_MODEL:3 * D_MODEL])
    vf_ref[...] = v
    vb_ref[...] = v.astype(BF16)
    lf_ref[...] = _log_sigmoid(_dot(hn, wf_ref[...]) + bf_ref[...])
    if with_kmean:
        for i in range(k.shape[0] // MOBA_BLOCK):
            blk = k[i * MOBA_BLOCK:(i + 1) * MOBA_BLOCK, 0:GROUP_WIDTH]
            km_ref[i] = jnp.mean(blk, axis=0, keepdims=True)


def project(h, g, w_qkv, w_f, b_f, with_kmean):
    rows, d = h.shape
    tm = min(rows, 512)
    row_spec = pl.BlockSpec((tm, d), lambda i: (i, 0))
    q_dtype = BF16 if with_kmean else F32
    out_shapes = [jax.ShapeDtypeStruct((rows, d), q_dtype),
                  jax.ShapeDtypeStruct((rows, d), F32), jax.ShapeDtypeStruct((rows, d), F32),
                  jax.ShapeDtypeStruct((rows, d), BF16), jax.ShapeDtypeStruct((rows, d), BF16),
                  jax.ShapeDtypeStruct((rows, LANES), F32)]
    out_specs = [row_spec] * 5 + [pl.BlockSpec((tm, LANES), lambda i: (i, 0))]
    if with_kmean:
        out_shapes.append(jax.ShapeDtypeStruct((rows // MOBA_BLOCK, 1, GROUP_WIDTH), F32))
        out_specs.append(pl.BlockSpec((tm // MOBA_BLOCK, 1, GROUP_WIDTH), lambda i: (i, 0, 0)))
    return pl.pallas_call(
        functools.partial(_proj_kernel, with_kmean),
        grid=(rows // tm,),
        in_specs=[
            row_spec,
            pl.BlockSpec((1, d), lambda i: (0, 0)),
            pl.BlockSpec((d, 3 * d), lambda i: (0, 0)),
            pl.BlockSpec((d, LANES), lambda i: (0, 0)),
            pl.BlockSpec((1, LANES), lambda i: (0, 0)),
        ],
        out_specs=out_specs,
        out_shape=out_shapes,
        compiler_params=_cparams(1),
        name="project",
    )(h, g, w_qkv, w_f, b_f)


def _cumsum_kernel(lf_ref, tri_ref, c_ref, ct_ref):
    n_chunks = lf_ref.shape[0] // LANES
    tri = tri_ref[...]

    def body(i, carry):
        rows = pl.ds(pl.multiple_of(i * LANES, LANES), LANES)
        c = _dot_f32_rhs(tri, lf_ref[rows, :]) + carry
        c_ref[rows, :] = c
        ct_ref[:, rows] = c.T[0:8, :]
        return c[LANES - 1:LANES, :]

    lax.fori_loop(0, n_chunks, body, jnp.zeros((1, LANES), F32))


def cumsum_logf(lf):
    rows = lf.shape[0]
    idx = jnp.arange(LANES)
    tri = (idx[None, :] <= idx[:, None]).astype(BF16)
    return pl.pallas_call(
        _cumsum_kernel,
        out_shape=[jax.ShapeDtypeStruct((rows, LANES), F32), jax.ShapeDtypeStruct((8, rows), F32)],
        compiler_params=pltpu.CompilerParams(vmem_limit_bytes=VMEM_LIMIT),
        name="cumsum_logf",
    )(lf, tri)


def _online_update(s, allowed, v, m_ref, l_ref, acc_ref, idx, guard_empty):
    if allowed is not None:
        s = jnp.where(allowed, s, NEG_INF)
    m_prev = m_ref[idx]
    m_new = jnp.maximum(m_prev, jnp.max(s, axis=-1, keepdims=True))
    p = jnp.exp(s - m_new)
    if guard_empty and allowed is not None:
        p = jnp.where(allowed, p, 0.0)
    alpha = jnp.exp(m_prev - m_new)
    l_ref[idx] = alpha * l_ref[idx] + jnp.sum(p, axis=-1, keepdims=True)
    acc_ref[idx] = alpha * acc_ref[idx] + _dot(p.astype(BF16), v)
    m_ref[idx] = m_new


def _attn_kernel(mode, tq, tk, qi_tab, kj_tab, *refs):
    if mode == G_MOBA:
        q_ref, k_ref, v_ref, km_ref, o_ref, qm_ref, m_ref, l_ref, acc_ref, sel_ref = refs
    elif mode == G_FOX:
        q_ref, k_ref, v_ref, c_ref, ct_ref, o_ref, qm_ref, m_ref, l_ref, acc_ref, cq_ref = refs
    elif mode == G_SB:
        q_ref, k_ref, v_ref, u_ref, o_ref, qm_ref, carry_ref, acc_ref = refs
    else:
        q_ref, k_ref, v_ref, lam_ref, sg_ref, o_ref, qm_ref, m_ref, l_ref, acc_ref = refs

    pair = pl.program_id(0)
    step_id = pl.program_id(1)
    qi = qi_tab[step_id]
    kj = kj_tab[step_id]
    last_k = (qi * tq + tq - 1) // tk
    kt = (last_k - kj) if mode == G_SB else kj
    lane_q = _lane_iota((tq, PAIR_WIDTH))
    qpos = qi * tq + _row_iota((tq, 1))
    kpos = kt * tk + _lane_iota((1, tk))
    slopes = MOBA_SLOPES if mode == G_MOBA else DIFF_SLOPES

    def head_slope(hl):
        return jnp.where(pair == 0, slopes[hl], slopes[2 + hl])

    @pl.when(kj == 0)
    def _init():
        acc_ref[...] = jnp.zeros_like(acc_ref)
        if mode == G_SB:
            carry_ref[...] = jnp.zeros_like(carry_ref)
        else:
            m_ref[...] = jnp.full_like(m_ref, NEG_INF)
            l_ref[...] = jnp.zeros_like(l_ref)
        q = q_ref[...].astype(F32)
        for hl in range(2):
            if mode == G_DIFF:
                for mp in range(2):
                    lo = hl * HEAD_DIM + mp * DIFF_QK_DIM
                    keep = (lane_q >= lo) & (lane_q < lo + DIFF_QK_DIM)
                    qm_ref[hl * 2 + mp] = jnp.where(keep, q, 0.0).astype(BF16)
            else:
                keep = (lane_q // HEAD_DIM) == hl
                qm_ref[hl] = jnp.where(keep, q * (HEAD_DIM ** -0.5), 0.0).astype(BF16)
        if mode == G_FOX:
            c = c_ref[...]
            lane = _lane_iota(c.shape)
            for hl in range(2):
                cq_ref[hl] = jnp.sum(jnp.where(lane == pair * 2 + hl, c, 0.0), axis=-1, keepdims=True)
        if mode == G_MOBA:
            lane_b = _lane_iota((tq, LANES))
            qblk = qpos // MOBA_BLOCK
            for hl in range(2):
                g = _dot_nt(qm_ref[hl], km_ref[...])
                g = jnp.where(lane_b < qblk, g, NEG_INF)
                sel = jnp.zeros((tq, LANES), F32)
                for _ in range(MOBA_TOPK):
                    mx = jnp.max(g, axis=-1, keepdims=True)
                    first = jnp.min(jnp.where(g == mx, lane_b, LANES), axis=-1, keepdims=True)
                    hit = lane_b == first
                    sel = jnp.where(hit, 1.0, sel)
                    g = jnp.where(hit, -jnp.inf, g)
                sel_ref[hl] = jnp.where(lane_b < qblk, sel, 0.0)

    def step(diag):
        k = k_ref[...]
        v = v_ref[...]
        causal = (kpos <= qpos) if diag else None
        for hl in range(2):
            if mode == G_DIFF:
                bias = -head_slope(hl) * (qpos - kpos).astype(F32)
                for mp in range(2):
                    s = _dot_nt(qm_ref[hl * 2 + mp], k) * (DIFF_QK_DIM ** -0.5) + bias
                    _online_update(s, causal, v, m_ref, l_ref, acc_ref, hl * 2 + mp, False)
                continue
            s = _dot_nt(qm_ref[hl], k)
            if mode == G_FOX:
                h = pair * 2 + hl
                s = s + cq_ref[hl] - ct_ref[pl.ds(h, 1), :]
                _online_update(s, causal, v, m_ref, l_ref, acc_ref, hl, False)
            elif mode == G_MOBA:
                s = s - head_slope(hl) * (qpos - kpos).astype(F32)
                sel = sel_ref[hl]
                lane_b = _lane_iota(sel.shape)
                qblk = qpos // MOBA_BLOCK
                kblk = kpos // MOBA_BLOCK
                picked = jnp.zeros((tq, tk), F32)
                for b in range(tk // MOBA_BLOCK):
                    blk = kt * (tk // MOBA_BLOCK) + b
                    col = jnp.sum(jnp.where(lane_b == blk, sel, 0.0), axis=-1, keepdims=True)
                    picked = jnp.where(kblk == blk, col, picked)
                allowed = (picked > 0.0) | ((qblk == kblk) & (kpos <= qpos))
                _online_update(s, allowed, v, m_ref, l_ref, acc_ref, hl, True)
            else:
                strict = (kpos < qpos) if diag else None
                lk = -_softplus(s)
                if diag:
                    lk = jnp.where(strict, lk, 0.0)
                lk_hi = lk.astype(BF16)
                lk_lo = (lk - lk_hi.astype(F32)).astype(BF16)
                rest = _dot(lk_hi, u_ref[...]) + _dot(lk_lo, u_ref[...])
                a = jnp.exp(s + lk + rest + carry_ref[hl])
                if diag:
                    a = jnp.where(strict, a, 0.0)
                acc_ref[hl] = acc_ref[hl] + _dot(a.astype(BF16), v)
                carry_ref[hl] = carry_ref[hl] + jnp.sum(lk, axis=-1, keepdims=True)

    if mode == G_MOBA:
        step(True)
    else:
        touches_diag = (kt * tk + tk - 1) > (qi * tq)

        @pl.when(touches_diag)
        def _diag():
            step(True)

        @pl.when(jnp.logical_not(touches_diag))
        def _interior():
            step(False)

    @pl.when(kj == last_k)
    def _finalize():
        first_head = (lane_q // HEAD_DIM) == 0
        if mode == G_SB:
            o = jnp.where(first_head, acc_ref[0], acc_ref[1])
        elif mode == G_DIFF:
            lp = lam_ref[...]
            lam_init = lp[4:5, 0:1]
            lam = (jnp.exp(jnp.sum(lp[0:1] * lp[1:2], axis=-1, keepdims=True))
                   - jnp.exp(jnp.sum(lp[2:3] * lp[3:4], axis=-1, keepdims=True)) + lam_init)
            outs = []
            for hl in range(2):
                o_h = acc_ref[2 * hl] / l_ref[2 * hl] - lam * (acc_ref[2 * hl + 1] / l_ref[2 * hl + 1])
                own = (lane_q // HEAD_DIM) == hl
                ms = jnp.sum(jnp.where(own, o_h * o_h, 0.0), axis=-1, keepdims=True) * (1.0 / HEAD_DIM)
                outs.append(o_h * lax.rsqrt(ms + RMS_EPS))
            o = jnp.where(first_head, outs[0], outs[1]) * sg_ref[...] * (1.0 - lam_init)
        else:
            o = jnp.where(first_head, acc_ref[0] / l_ref[0], acc_ref[1] / l_ref[1])
        o_ref[...] = o.astype(o_ref.dtype)


def _lam_params(lq1, lk1, lq2, lk2, lam_init):
    rows = [jnp.pad(a.astype(F32), (0, LANES - a.shape[0])) for a in (lq1, lk1, lq2, lk2)]
    rows.append(jnp.full((LANES,), lam_init, F32))
    rows += [jnp.zeros((LANES,), F32)] * 3
    return jnp.stack(rows)


def prompt_attention(mode, q, k, v, extras, tq=512, tk=512):
    t = q.shape[0]
    tq = min(tq, t)
    tk = min(tk, t)
    nq = t // tq
    col = mode * 2

    def last_tile(qi):
        return (qi * tq + tq - 1) // tk

    qi_list, kj_list = [], []
    for qi in range(nq):
        for kj in range(last_tile(qi) + 1):
            qi_list.append(qi)
            kj_list.append(kj)
    qi_tab = jnp.asarray(np.array(qi_list, np.int32))
    kj_tab = jnp.asarray(np.array(kj_list, np.int32))

    def key_tile(s, qt, kt):
        if mode == G_SB:
            return last_tile(qt[s]) - kt[s]
        return kt[s]

    q_map = lambda p, s, qt, kt: (qt[s], col + p)
    kv_map = lambda p, s, qt, kt: (key_tile(s, qt, kt), col + p)
    const = lambda p, s, qt, kt: (0, 0)

    in_specs = [pl.BlockSpec((tq, PAIR_WIDTH), q_map),
                pl.BlockSpec((tk, PAIR_WIDTH), kv_map),
                pl.BlockSpec((tk, PAIR_WIDTH), kv_map)]
    n_state = 4 if mode == G_DIFF else 2
    qm = pltpu.VMEM((n_state, tq, PAIR_WIDTH), BF16)
    stat = pltpu.VMEM((n_state, tq, 1), F32)
    acc = pltpu.VMEM((n_state, tq, PAIR_WIDTH), F32)
    if mode == G_MOBA:
        in_specs.append(pl.BlockSpec((LANES, PAIR_WIDTH), lambda p, s, qt, kt: (0, p)))
        scratch = [qm, stat, stat, acc, pltpu.VMEM((2, tq, LANES), F32)]
    elif mode == G_FOX:
        in_specs.append(pl.BlockSpec((tq, LANES), lambda p, s, qt, kt: (qt[s], 0)))
        in_specs.append(pl.BlockSpec((8, tk), lambda p, s, qt, kt: (0, kt[s])))
        scratch = [qm, stat, stat, acc, stat]
    elif mode == G_SB:
        idx = jnp.arange(tk)
        extras = ((idx[:, None] > idx[None, :]).astype(BF16),)
        in_specs.append(pl.BlockSpec((tk, tk), const))
        scratch = [qm, stat, acc]
    else:
        in_specs.append(pl.BlockSpec((8, LANES), const))
        in_specs.append(pl.BlockSpec((1, PAIR_WIDTH), const))
        scratch = [qm, stat, stat, acc]
    grid_spec = pltpu.PrefetchScalarGridSpec(
        num_scalar_prefetch=2,
        grid=(2, len(qi_list)),
        in_specs=in_specs,
        out_specs=pl.BlockSpec((tq, PAIR_WIDTH), lambda p, s, qt, kt: (qt[s], p)),
        scratch_shapes=scratch,
    )
    return pl.pallas_call(
        functools.partial(_attn_kernel, mode, tq, tk),
        grid_spec=grid_spec,
        out_shape=jax.ShapeDtypeStruct((t, GROUP_WIDTH), BF16),
        compiler_params=_cparams(2),
        name="prompt_attn_%d" % mode,
    )(qi_tab, kj_tab, q, k, v, *extras)


def _outproj_kernel(h_ref, o0_ref, o1_ref, o2_ref, o3_ref, w_ref, g_ref, out_ref):
    m = _dot(o0_ref[...], w_ref[0:GROUP_WIDTH, :])
    for gi, o_ref in enumerate((o1_ref, o2_ref, o3_ref), start=1):
        m = m + _dot(o_ref[...], w_ref[gi * GROUP_WIDTH:(gi + 1) * GROUP_WIDTH, :])
    out_ref[...] = h_ref[...] + _rms(m, g_ref[...])


def out_project(h, outs, w_out, g):
    rows, d = h.shape
    tm = min(rows, 1024)
    o_spec = pl.BlockSpec((tm, GROUP_WIDTH), lambda i: (i, 0))
    return pl.pallas_call(
        _outproj_kernel,
        grid=(rows // tm,),
        in_specs=[pl.BlockSpec((tm, d), lambda i: (i, 0)), o_spec, o_spec, o_spec, o_spec,
                  pl.BlockSpec((d, d), lambda i: (0, 0)), pl.BlockSpec((1, d), lambda i: (0, 0))],
        out_specs=pl.BlockSpec((tm, d), lambda i: (i, 0)),
        out_shape=jax.ShapeDtypeStruct((rows, d), F32),
        compiler_params=_cparams(1),
        name="out_project",
    )(h, *outs, w_out, g)


ROWS = 32
S_FOX, S_D1, S_D2, S_SB = 0, 1, 2, 3


def _sample_kernel(n_pages, page, dec_seq,
                   pt_ref, q_ref, kn_ref, vn_ref, lfn_ref, lfnt_ref, ck_ref, cv_ref, clt_ref,
                   u_ref, lam_ref, sg_ref, o_ref,
                   qs_ref, newk_ref, newv_ref, m_ref, l_ref, acc_ref, sbc_ref, fcar_ref, crow_ref,
                   bm_ref, bl_ref, bg_ref, bacc_ref):
    j = pl.program_id(1)
    n_blocks = (n_pages * page) // MOBA_BLOCK
    pages_per_block = MOBA_BLOCK // page
    past_len = n_pages * page
    pp = n_pages - j
    row = _row_iota((ROWS, 1))
    row_head = row // dec_seq
    qpos = past_len + (row % dec_seq)
    kpos = pp * page + _lane_iota((1, page))
    lane_g = _lane_iota((ROWS, GROUP_WIDTH))
    own_lanes = (lane_g // HEAD_DIM) == row_head

    @pl.when(j == 0)
    def _init():
        q = q_ref[...]
        for c in range(4):
            qc = q[:, c * GROUP_WIDTH:(c + 1) * GROUP_WIDTH]
            qt = jnp.concatenate([qc] * GROUP_HEADS, axis=0)
            if c == G_DIFF:
                sub = lane_g % HEAD_DIM
                qs_ref[3] = jnp.where(own_lanes & (sub < DIFF_QK_DIM), qt, 0.0).astype(BF16)
                qs_ref[4] = jnp.where(own_lanes & (sub >= DIFF_QK_DIM), qt, 0.0).astype(BF16)
            else:
                qs_ref[c] = jnp.where(own_lanes, qt * (HEAD_DIM ** -0.5), 0.0).astype(BF16)
        pad = jnp.zeros((page - dec_seq, kn_ref.shape[-1]), F32)
        newk_ref[...] = jnp.concatenate([kn_ref[...], pad], axis=0).astype(BF16)
        newv_ref[...] = jnp.concatenate([vn_ref[...], pad], axis=0).astype(BF16)
        m_ref[...] = jnp.full_like(m_ref, NEG_INF)
        l_ref[...] = jnp.zeros_like(l_ref)
        acc_ref[...] = jnp.zeros_like(acc_ref)
        sbc_ref[...] = jnp.zeros_like(sbc_ref)
        lfn = lfn_ref[...]
        tok = _row_iota(lfn.shape)
        cum = jnp.zeros_like(lfn)
        for i in range(dec_seq):
            cum = cum + jnp.where(tok >= i, lfn[i:i + 1, :], 0.0)
        cum_t = jnp.concatenate([cum] * GROUP_HEADS, axis=0)
        lane = _lane_iota(cum_t.shape)
        crow_ref[...] = jnp.sum(jnp.where(lane == row_head, cum_t, 0.0), axis=-1, keepdims=True)
        fcar_ref[...] = -jnp.sum(lfnt_ref[...], axis=-1, keepdims=True)

    def page_step(kb, vb, lft):
        causal = kpos <= qpos
        dist = (qpos - kpos).astype(F32)

        s_raw = _dot_nt(qs_ref[G_MOBA], kb[:, 0:GROUP_WIDTH])
        s = jnp.where(causal, s_raw - _select_by_index(row_head, MOBA_SLOPES) * dist, NEG_INF)
        blk = jnp.where(j == 0, n_blocks, pp // pages_per_block)
        fresh = (j == 0) | ((pp % pages_per_block) == (pages_per_block - 1))
        m_prev = jnp.where(fresh, NEG_INF, bm_ref[blk])
        l_prev = jnp.where(fresh, 0.0, bl_ref[blk])
        g_prev = jnp.where(fresh, 0.0, bg_ref[blk])
        a_prev = jnp.where(fresh, 0.0, bacc_ref[blk])
        m_new = jnp.maximum(m_prev, jnp.max(s, axis=-1, keepdims=True))
        p = jnp.exp(s - m_new)
        alpha = jnp.exp(m_prev - m_new)
        bm_ref[blk] = m_new
        bl_ref[blk] = alpha * l_prev + jnp.sum(p, axis=-1, keepdims=True)
        bg_ref[blk] = g_prev + jnp.sum(s_raw, axis=-1, keepdims=True)
        bacc_ref[blk] = alpha * a_prev + _dot(p.astype(BF16), vb[:, 0:GROUP_WIDTH])

        suffix = _dot_f32_lhs(lft, u_ref[...]) + fcar_ref[...]
        suffix_rows = jnp.concatenate(
            [jnp.broadcast_to(suffix[h:h + 1, :], (dec_seq, page)) for h in range(GROUP_HEADS)], axis=0)
        s = _dot_nt(qs_ref[G_FOX], kb[:, GROUP_WIDTH:2 * GROUP_WIDTH]) + crow_ref[...] + suffix_rows
        _online_update(s, causal, vb[:, GROUP_WIDTH:2 * GROUP_WIDTH], m_ref, l_ref, acc_ref, S_FOX, False)
        fcar_ref[...] = fcar_ref[...] + jnp.sum(lft, axis=-1, keepdims=True)

        z = _dot_nt(qs_ref[G_SB], kb[:, 2 * GROUP_WIDTH:3 * GROUP_WIDTH])
        strict = kpos < qpos
        lk = jnp.where(strict, -_softplus(z), 0.0)
        lk_hi = lk.astype(BF16)
        lk_lo = (lk - lk_hi.astype(F32)).astype(BF16)
        rest = _dot(lk_hi, u_ref[...]) + _dot(lk_lo, u_ref[...])
        a = jnp.where(strict, jnp.exp(z + lk + rest + sbc_ref[...]), 0.0)
        acc_ref[S_SB] = acc_ref[S_SB] + _dot(a.astype(BF16), vb[:, 2 * GROUP_WIDTH:3 * GROUP_WIDTH])
        sbc_ref[...] = sbc_ref[...] + jnp.sum(lk, axis=-1, keepdims=True)

        bias_d = -_select_by_index(row_head, DIFF_SLOPES) * dist
        kd = kb[:, 3 * GROUP_WIDTH:4 * GROUP_WIDTH]
        vd = vb[:, 3 * GROUP_WIDTH:4 * GROUP_WIDTH]
        for mp, slot in ((0, S_D1), (1, S_D2)):
            s = _dot_nt(qs_ref[3 + mp], kd) * (DIFF_QK_DIM ** -0.5) + bias_d
            _online_update(s, causal, vd, m_ref, l_ref, acc_ref, slot, False)

    @pl.when(j == 0)
    def _new_tokens():
        page_step(newk_ref[...], newv_ref[...], lfnt_ref[...])

    @pl.when(j > 0)
    def _past_page():
        page_step(ck_ref[...].astype(BF16), cv_ref[...].astype(BF16), clt_ref[...])

    @pl.when(j == n_pages)
    def _finalize():
        def gather_heads(o):
            o = jnp.where(own_lanes, o, 0.0)
            out = o[0:dec_seq]
            for h in range(1, GROUP_HEADS):
                out = out + o[h * dec_seq:(h + 1) * dec_seq]
            return out

        gates = bg_ref[0:n_blocks]
        bidx = lax.broadcasted_iota(jnp.int32, gates.shape, 0)
        sel = jnp.zeros(gates.shape, F32)
        for _ in range(min(MOBA_TOPK, n_blocks)):
            mx = jnp.max(gates, axis=0, keepdims=True)
            first = jnp.min(jnp.where(gates == mx, bidx, n_blocks), axis=0, keepdims=True)
            hit = bidx == first
            sel = jnp.where(hit, 1.0, sel)
            gates = jnp.where(hit, -jnp.inf, gates)
        picked = sel > 0.0
        bm = bm_ref[0:n_blocks]
        m_own = bm_ref[n_blocks]
        m_fin = jnp.maximum(m_own, jnp.max(jnp.where(picked, bm, NEG_INF), axis=0))
        w = jnp.where(picked, jnp.exp(bm - m_fin), 0.0)
        w_own = jnp.exp(m_own - m_fin)
        den = w_own * bl_ref[n_blocks] + jnp.sum(w * bl_ref[0:n_blocks], axis=0)
        num = w_own * bacc_ref[n_blocks] + jnp.sum(w * bacc_ref[0:n_blocks], axis=0)
        o_ref[:, 0:GROUP_WIDTH] = gather_heads(num / den)

        o_ref[:, GROUP_WIDTH:2 * GROUP_WIDTH] = gather_heads(acc_ref[S_FOX] / l_ref[S_FOX])
        o_ref[:, 2 * GROUP_WIDTH:3 * GROUP_WIDTH] = gather_heads(acc_ref[S_SB])

        lp = lam_ref[...]
        lam_init = lp[4:5, 0:1]
        lam = (jnp.exp(jnp.sum(lp[0:1] * lp[1:2], axis=-1, keepdims=True))
               - jnp.exp(jnp.sum(lp[2:3] * lp[3:4], axis=-1, keepdims=True)) + lam_init)
        o_d = acc_ref[S_D1] / l_ref[S_D1] - lam * (acc_ref[S_D2] / l_ref[S_D2])
        ms = jnp.sum(jnp.where(own_lanes, o_d * o_d, 0.0), axis=-1, keepdims=True) * (1.0 / HEAD_DIM)
        o_d = o_d * lax.rsqrt(ms + RMS_EPS) * sg_ref[...] * (1.0 - lam_init)
        o_ref[:, 3 * GROUP_WIDTH:4 * GROUP_WIDTH] = gather_heads(o_d)


def sample_attention(layer, q, k_new, v_new, lf_new, cache_k, cache_v, cache_lft, page_table, lam_p, subln):
    b, dec_seq, d = q.shape
    n_pages = page_table.shape[1]
    page = cache_k.shape[2]
    n_blocks = (n_pages * page) // MOBA_BLOCK
    assert ROWS == GROUP_HEADS * dec_seq and page == LANES and MOBA_BLOCK % page == 0
    lf_new_t = jnp.pad(jnp.swapaxes(lf_new[:, :, 0:GROUP_HEADS], 1, 2),
                       ((0, 0), (0, 8 - GROUP_HEADS), (0, page - dec_seq)))
    idx = jnp.arange(page)
    u = (idx[:, None] > idx[None, :]).astype(BF16)
    subln4 = jnp.tile(subln, (1, GROUP_HEADS // 2))

    def page_idx(bi, j, pt):
        return pt[bi, n_pages - jnp.maximum(j, 1)]

    per_batch = lambda bi, j, pt: (bi, 0, 0)
    const2 = lambda bi, j, pt: (0, 0)
    grid_spec = pltpu.PrefetchScalarGridSpec(
        num_scalar_prefetch=1,
        grid=(b, n_pages + 1),
        in_specs=[
            pl.BlockSpec((None, dec_seq, d), per_batch),
            pl.BlockSpec((None, dec_seq, d), per_batch),
            pl.BlockSpec((None, dec_seq, d), per_batch),
            pl.BlockSpec((None, dec_seq, LANES), per_batch),
            pl.BlockSpec((None, 8, page), per_batch),
            pl.BlockSpec((None, None, page, d), lambda bi, j, pt: (layer, page_idx(bi, j, pt), 0, 0)),
            pl.BlockSpec((None, None, page, d), lambda bi, j, pt: (layer, page_idx(bi, j, pt), 0, 0)),
            pl.BlockSpec((None, None, 8, page), lambda bi, j, pt: (layer, page_idx(bi, j, pt), 0, 0)),
            pl.BlockSpec((page, page), const2),
            pl.BlockSpec((8, LANES), const2),
            pl.BlockSpec((1, GROUP_WIDTH), const2),
        ],
        out_specs=pl.BlockSpec((None, dec_seq, d), per_batch),
        scratch_shapes=[
            pltpu.VMEM((5, ROWS, GROUP_WIDTH), BF16),
            pltpu.VMEM((page, d), BF16), pltpu.VMEM((page, d), BF16),
            pltpu.VMEM((3, ROWS, 1), F32), pltpu.VMEM((3, ROWS, 1), F32),
            pltpu.VMEM((4, ROWS, GROUP_WIDTH), F32),
            pltpu.VMEM((ROWS, 1), F32), pltpu.VMEM((8, 1), F32), pltpu.VMEM((ROWS, 1), F32),
            pltpu.VMEM((n_blocks + 1, ROWS, 1), F32), pltpu.VMEM((n_blocks + 1, ROWS, 1), F32),
            pltpu.VMEM((n_blocks + 1, ROWS, 1), F32), pltpu.VMEM((n_blocks + 1, ROWS, GROUP_WIDTH), F32),
        ],
    )
    return pl.pallas_call(
        functools.partial(_sample_kernel, n_pages, page, dec_seq),
        grid_spec=grid_spec,
        out_shape=jax.ShapeDtypeStruct((b, dec_seq, d), F32),
        compiler_params=_cparams(2),
        name="sample_attn",
    )(page_table, q, k_new, v_new, lf_new, lf_new_t, cache_k, cache_v, cache_lft, u, lam_p, subln4)


def kernel(x_prompt, x_sample, cache_k, cache_v, cache_logf, page_table, norm_g, w_in, b_f, w_out,
           diff_lq1, diff_lk1, diff_lq2, diff_lk2, diff_subln_g, ffn_w_gate, ffn_w_up, ffn_w_down):
    depth = norm_g.shape[0]
    bp, seq, d = x_prompt.shape
    bs, dec_seq, _ = x_sample.shape
    n_phys, page = cache_k.shape[1], cache_k.shape[2]
    assert bp == 1 and d == D_MODEL
    mix = N_HEADS * HEAD_DIM

    ck = cache_k.reshape(depth, n_phys, page, mix)
    cv = cache_v.reshape(depth, n_phys, page, mix)
    clt = jnp.pad(jnp.swapaxes(cache_logf, 2, 3), ((0, 0), (0, 0), (0, 8 - GROUP_HEADS), (0, 0)))

    hp = x_prompt.reshape(seq, d)
    hs = x_sample.reshape(bs * dec_seq, d)
    pk, pv, pf, sk, sv, sf = [], [], [], [], [], []
    for l in range(depth):
        lam_init = 0.8 - 0.6 * math.exp(-0.3 * l)
        g = norm_g[l].reshape(6, 1, d)
        wg = ffn_w_gate[l].astype(BF16)
        wu = ffn_w_up[l].astype(BF16)
        wd = ffn_w_down[l].astype(BF16)
        w_qkv = w_in[l, :, 0:3 * mix].astype(BF16)
        w_f = jnp.pad(w_in[l, :, 3 * mix:], ((0, 0), (0, LANES - GROUP_HEADS))).astype(BF16)
        bias_f = jnp.pad(b_f[l].astype(F32), (0, LANES - GROUP_HEADS)).reshape(1, LANES)
        wo = w_out[l].astype(BF16)
        lam_p = _lam_params(diff_lq1[l], diff_lk1[l], diff_lq2[l], diff_lk2[l], lam_init)
        subln = jnp.tile(diff_subln_g[l].astype(F32).reshape(1, HEAD_DIM), (1, 2))

        hp = ffn_block(hp, g[0], g[1], wg[0], wu[0], wd[0])
        q, kf, vf, kb, vb, lf, kmean = project(hp, g[2], w_qkv, w_f, bias_f, True)
        c, ct = cumsum_logf(lf)
        n_blk = seq // MOBA_BLOCK
        km = jnp.pad(kmean.reshape(n_blk, GROUP_WIDTH), ((0, LANES - n_blk), (0, 0))).astype(BF16)
        outs = [prompt_attention(G_MOBA, q, kb, vb, (km,)),
                prompt_attention(G_FOX, q, kb, vb, (c, ct)),
                prompt_attention(G_SB, q, kb, vb, ()),
                prompt_attention(G_DIFF, q, kb, vb, (lam_p, subln))]
        hp = out_project(hp, outs, wo, g[3])
        hp = ffn_block(hp, g[4], g[5], wg[1], wu[1], wd[1])
        pk.append(kf.reshape(bp, seq, N_HEADS, HEAD_DIM))
        pv.append(vf.reshape(bp, seq, N_HEADS, HEAD_DIM))
        pf.append(lf[:, 0:GROUP_HEADS].reshape(bp, seq, GROUP_HEADS))

        hs = ffn_block(hs, g[0], g[1], wg[0], wu[0], wd[0])
        q, kf, vf, kb, vb, lf = project(hs, g[2], w_qkv, w_f, bias_f, False)
        o = sample_attention(l, q.reshape(bs, dec_seq, d), kf.reshape(bs, dec_seq, d),
                             vf.reshape(bs, dec_seq, d), lf.reshape(bs, dec_seq, LANES),
                             ck, cv, clt, page_table, lam_p, subln)
        o = o.reshape(bs * dec_seq, d).astype(BF16)
        outs = [o[:, i * GROUP_WIDTH:(i + 1) * GROUP_WIDTH] for i in range(4)]
        hs = out_project(hs, outs, wo, g[3])
        hs = ffn_block(hs, g[4], g[5], wg[1], wu[1], wd[1])
        sk.append(kf.reshape(bs, dec_seq, N_HEADS, HEAD_DIM))
        sv.append(vf.reshape(bs, dec_seq, N_HEADS, HEAD_DIM))
        sf.append(lf[:, 0:GROUP_HEADS].reshape(bs, dec_seq, GROUP_HEADS))

    return (hp.reshape(bp, seq, d), hs.reshape(bs, dec_seq, d),
            jnp.stack(pk), jnp.stack(pv), jnp.stack(pf), jnp.stack(sk), jnp.stack(sv), jnp.stack(sf))
```

```python
import functools
import math

import numpy as np
import jax
import jax.numpy as jnp
from jax import lax
from jax.experimental import pallas as pl
from jax.experimental.pallas import tpu as pltpu

F32 = jnp.float32
BF16 = jnp.bfloat16

D_MODEL = 1024
N_HEADS = 16
HEAD_DIM = 64
GROUP_HEADS = 4
GROUP_WIDTH = GROUP_HEADS * HEAD_DIM
PAIR_WIDTH = 2 * HEAD_DIM
DIFF_QK_DIM = HEAD_DIM // 2
MOBA_BLOCK = 256
MOBA_TOPK = 3
RMS_EPS = 1e-6
NEG_INF = -1e30
LANES = 128
G_MOBA, G_FOX, G_SB, G_DIFF = 0, 1, 2, 3
MOBA_SLOPES = tuple(2.0 ** -(2 * h + 1) for h in range(GROUP_HEADS))
DIFF_SLOPES = tuple(2.0 ** -(2 * h + 2) for h in range(GROUP_HEADS))
VMEM_LIMIT = 56 * 1024 * 1024
LOG2E = 1.4426950408889634
ONES_ROWS = 16
SB_DEAD = -104.0


def _cparams(n_axes):
    return pltpu.CompilerParams(dimension_semantics=("arbitrary",) * n_axes,
                                vmem_limit_bytes=VMEM_LIMIT)


def _rms(x, g):
    return x * lax.rsqrt(jnp.mean(x * x, axis=-1, keepdims=True) + RMS_EPS) * g


def _dot(a, b):
    return jnp.dot(a, b, preferred_element_type=F32)


def _dot_nt(a, b):
    return lax.dot_general(a, b, (((1,), (1,)), ((), ())), preferred_element_type=F32)


def _split3(a):
    a1 = a.astype(BF16)
    r1 = a - a1.astype(F32)
    a2 = r1.astype(BF16)
    a3 = (r1 - a2.astype(F32)).astype(BF16)
    return a1, a2, a3


def _dot_f32_lhs(a, b01):
    a1, a2, a3 = _split3(a)
    return _dot(a1, b01) + _dot(a2, b01) + _dot(a3, b01)


def _dot_f32_rhs(a01, b):
    b1, b2, b3 = _split3(b)
    return _dot(a01, b1) + _dot(a01, b2) + _dot(a01, b3)


def _log_sigmoid(x):
    return jnp.minimum(x, 0.0) - jnp.log1p(jnp.exp(-jnp.abs(x)))


def _softplus(x):
    return jnp.maximum(x, 0.0) + jnp.log1p(jnp.exp(-jnp.abs(x)))


def _lane_iota(shape):
    return lax.broadcasted_iota(jnp.int32, shape, len(shape) - 1)


def _row_iota(shape):
    return lax.broadcasted_iota(jnp.int32, shape, len(shape) - 2)


def _select_by_index(idx, values):
    out = jnp.full(jnp.shape(idx), values[-1], F32)
    for i in range(len(values) - 2, -1, -1):
        out = jnp.where(idx == i, values[i], out)
    return out


def _ffn_kernel(x_ref, gpre_ref, gpost_ref, wg_ref, wu_ref, wd_ref, o_ref, xn_ref, acc_ref):
    j = pl.program_id(1)

    @pl.when(j == 0)
    def _():
        xn_ref[...] = _rms(x_ref[...], gpre_ref[...]).astype(BF16)
        acc_ref[...] = jnp.zeros_like(acc_ref)

    xn = xn_ref[...]
    gate = _dot(xn, wg_ref[...])
    up = _dot(xn, wu_ref[...])
    mid = gate * jax.nn.sigmoid(gate) * up
    acc_ref[...] += _dot(mid.astype(BF16), wd_ref[...])

    @pl.when(j == pl.num_programs(1) - 1)
    def _():
        o_ref[...] = x_ref[...] + 0.5 * _rms(acc_ref[...], gpost_ref[...])


def _ffn_tiles(rows, d_ff):
    tm = min(rows, 512)
    tf = d_ff
    for cand in (1408, 1024, 512, 256, 128):
        if d_ff % cand == 0:
            tf = cand
            break
    return tm, tf


def ffn_block(x, g_pre, g_post, w_gate, w_up, w_down):
    rows, d = x.shape
    d_ff = w_gate.shape[1]
    tm, tf = _ffn_tiles(rows, d_ff)
    return pl.pallas_call(
        _ffn_kernel,
        grid=(rows // tm, d_ff // tf),
        in_specs=[
            pl.BlockSpec((tm, d), lambda i, j: (i, 0)),
            pl.BlockSpec((1, d), lambda i, j: (0, 0)),
            pl.BlockSpec((1, d), lambda i, j: (0, 0)),
            pl.BlockSpec((d, tf), lambda i, j: (0, j)),
            pl.BlockSpec((d, tf), lambda i, j: (0, j)),
            pl.BlockSpec((tf, d), lambda i, j: (j, 0)),
        ],
        out_specs=pl.BlockSpec((tm, d), lambda i, j: (i, 0)),
        out_shape=jax.ShapeDtypeStruct((rows, d), F32),
        scratch_shapes=[pltpu.VMEM((tm, d), BF16), pltpu.VMEM((tm, d), F32)],
        compiler_params=_cparams(2),
        name="ffn_block",
    )(x, g_pre, g_post, w_gate, w_up, w_down)


def _proj_kernel(with_kmean, h_ref, g_ref, w_ref, wf_ref, bf_ref, *out_refs):
    if with_kmean:
        q_ref, kf_ref, vf_ref, kb_ref, vb_ref, lf_ref, km_ref = out_refs
    else:
        q_ref, kf_ref, vf_ref, kb_ref, vb_ref, lf_ref = out_refs
    hn = _rms(h_ref[...], g_ref[...]).astype(BF16)
    q = _dot(hn, w_ref[:, 0:D_MODEL])
    q_ref[...] = q.astype(q_ref.dtype)
    k = _dot(hn, w_ref[:, D_MODEL:2 * D_MODEL])
    kf_ref[...] = k
    kb_ref[...] = k.astype(BF16)
    v = _dot(hn, w_ref[:, 2 * D_MODEL:3 * D_MODEL])
    vf_ref[...] = v
    vb_ref[...] = (v.T if with_kmean else v).astype(BF16)
    lf_ref[...] = _log_sigmoid(_dot(hn, wf_ref[...]) + bf_ref[...])
    if with_kmean:
        for i in range(k.shape[0] // MOBA_BLOCK):
            blk = k[i * MOBA_BLOCK:(i + 1) * MOBA_BLOCK, 0:GROUP_WIDTH]
            km_ref[i] = jnp.mean(blk, axis=0, keepdims=True)


def project(h, g, w_qkv, w_f, b_f, with_kmean):
    rows, d = h.shape
    tm = min(rows, 512)
    row_spec = pl.BlockSpec((tm, d), lambda i: (i, 0))
    q_dtype = BF16 if with_kmean else F32
    out_shapes = [jax.ShapeDtypeStruct((rows, d), q_dtype),
                  jax.ShapeDtypeStruct((rows, d), F32), jax.ShapeDtypeStruct((rows, d), F32),
                  jax.ShapeDtypeStruct((rows, d), BF16),
                  jax.ShapeDtypeStruct((d, rows) if with_kmean else (rows, d), BF16),
                  jax.ShapeDtypeStruct((rows, LANES), F32)]
    v_spec = pl.BlockSpec((d, tm), lambda i: (0, i)) if with_kmean else row_spec
    out_specs = [row_spec] * 4 + [v_spec, pl.BlockSpec((tm, LANES), lambda i: (i, 0))]
    if with_kmean:
        out_shapes.append(jax.ShapeDtypeStruct((rows // MOBA_BLOCK, 1, GROUP_WIDTH), F32))
        out_specs.append(pl.BlockSpec((tm // MOBA_BLOCK, 1, GROUP_WIDTH), lambda i: (i, 0, 0)))
    return pl.pallas_call(
        functools.partial(_proj_kernel, with_kmean),
        grid=(rows // tm,),
        in_specs=[
            row_spec,
            pl.BlockSpec((1, d), lambda i: (0, 0)),
            pl.BlockSpec((d, 3 * d), lambda i: (0, 0)),
            pl.BlockSpec((d, LANES), lambda i: (0, 0)),
            pl.BlockSpec((1, LANES), lambda i: (0, 0)),
        ],
        out_specs=out_specs,
        out_shape=out_shapes,
        compiler_params=_cparams(1),
        name="project",
    )(h, g, w_qkv, w_f, b_f)


def _cumsum_kernel(lf_ref, tri_ref, p1_ref, p2_ref, p3_ref):
    n_chunks = lf_ref.shape[0] // LANES
    tri = tri_ref[...]

    def body(i, carry):
        rows = pl.ds(pl.multiple_of(i * LANES, LANES), LANES)
        c = _dot_f32_rhs(tri, lf_ref[rows, :]) + carry
        p1, p2, p3 = _split3(c * LOG2E)
        p1_ref[rows, :] = p1
        p2_ref[rows, :] = p2
        p3_ref[rows, :] = p3
        return c[LANES - 1:LANES, :]

    lax.fori_loop(0, n_chunks, body, jnp.zeros((1, LANES), F32))


def cumsum_logf(lf):
    rows = lf.shape[0]
    idx = jnp.arange(LANES)
    tri = (idx[None, :] <= idx[:, None]).astype(BF16)
    piece = jax.ShapeDtypeStruct((rows, LANES), BF16)
    return pl.pallas_call(
        _cumsum_kernel,
        out_shape=[piece, piece, piece],
        compiler_params=pltpu.CompilerParams(vmem_limit_bytes=VMEM_LIMIT),
        name="cumsum_logf",
    )(lf, tri)


def _online_update(s, allowed, v, m_ref, l_ref, acc_ref, idx, guard_empty):
    if allowed is not None:
        s = jnp.where(allowed, s, NEG_INF)
    m_prev = m_ref[idx]
    m_new = jnp.maximum(m_prev, jnp.max(s, axis=-1, keepdims=True))
    p = jnp.exp(s - m_new)
    if guard_empty and allowed is not None:
        p = jnp.where(allowed, p, 0.0)
    alpha = jnp.exp(m_prev - m_new)
    l_ref[idx] = alpha * l_ref[idx] + jnp.sum(p, axis=-1, keepdims=True)
    acc_ref[idx] = alpha * acc_ref[idx] + _dot(p.astype(BF16), v)
    m_ref[idx] = m_new


def _split3_const(x):
    x = np.float32(x)
    a1 = np.float32(np.asarray(x, dtype=BF16))
    a2 = np.float32(np.asarray(x - a1, dtype=BF16))
    a3 = np.float32(np.asarray(x - a1 - a2, dtype=BF16))
    return float(a1), float(a2), float(a3)


def _aug_base(h):
    return HEAD_DIM if h % 2 == 0 else 0


def alibi_aug(t, slopes):
    pos = jnp.arange(t, dtype=jnp.int32)
    hi = ((pos // LANES) * LANES).astype(F32)
    lo = (pos % LANES).astype(F32)
    qa_all, ka_all = [], []
    for h, slope in enumerate(slopes):
        c = _split3_const(LOG2E * slope)
        base = _aug_base(h)
        qa = jnp.zeros((t, LANES), F32)
        ka = jnp.zeros((t, LANES), F32)
        for i in range(3):
            qa = qa.at[:, base + i].set(c[i]).at[:, base + 3 + i].set(c[i])
            ka = ka.at[:, base + i].set(hi).at[:, base + 3 + i].set(lo)
            qa = qa.at[:, base + 6 + i].set(-hi).at[:, base + 9 + i].set(-lo)
            ka = ka.at[:, base + 6 + i].set(c[i]).at[:, base + 9 + i].set(c[i])
        qa_all.append(qa)
        ka_all.append(ka)
    return jnp.stack(qa_all).astype(BF16), jnp.stack(ka_all).astype(BF16)


def fox_aug(p1, p2, p3):
    t = p1.shape[0]
    one = jnp.ones((t,), BF16)
    qa_all, ka_all = [], []
    for h in range(GROUP_HEADS):
        base = _aug_base(h)
        qa = jnp.zeros((t, LANES), BF16)
        ka = jnp.zeros((t, LANES), BF16)
        for i, p in enumerate((p1, p2, p3)):
            qa = qa.at[:, base + i].set(p[:, h]).at[:, base + 3 + i].set(-one)
            ka = ka.at[:, base + i].set(one).at[:, base + 3 + i].set(p[:, h])
        qa_all.append(qa)
        ka_all.append(ka)
    return jnp.stack(qa_all), jnp.stack(ka_all)


def _attn_kernel(mode, tq, tk, qi_tab, kj_tab, *refs):
    if mode == G_MOBA:
        q_ref, k_ref, vt_ref, qa_ref, ka_ref, km_ref, o_ref, qaug_ref, m_ref, acc_ref, sel_ref = refs
    elif mode == G_FOX:
        q_ref, k_ref, vt_ref, qa_ref, ka_ref, o_ref, qaug_ref, m_ref, acc_ref = refs
    elif mode == G_SB:
        q_ref, k_ref, vt_ref, u_ref, o_ref, qaug_ref, carry_ref, acc_ref = refs
    else:
        q_ref, k_ref, vt_ref, qa_ref, ka_ref, lam_ref, sg_ref, o_ref, qaug_ref, m_ref, acc_ref = refs

    step_id = pl.program_id(1)
    qi = qi_tab[step_id]
    kj = kj_tab[step_id]
    last_k = (qi * tq + tq - 1) // tk
    kt = last_k - kj
    lane_q = _lane_iota((tq, PAIR_WIDTH))
    lane_row = _lane_iota((1, PAIR_WIDTH))
    qpos = qi * tq + _lane_iota((1, tq))

    @pl.when(kj == 0)
    def _init():
        acc_ref[...] = jnp.zeros_like(acc_ref)
        if mode == G_SB:
            carry_ref[...] = jnp.zeros_like(carry_ref)
        else:
            m_ref[...] = jnp.full_like(m_ref, NEG_INF)
        q = q_ref[...].astype(F32)
        for hl in range(2):
            own = (lane_q // HEAD_DIM) == hl
            if mode == G_SB:
                qaug_ref[hl] = jnp.where(own, q * (HEAD_DIM ** -0.5), 0.0).astype(BF16)
            elif mode == G_DIFF:
                qa = qa_ref[hl].astype(F32)
                qs = q * (DIFF_QK_DIM ** -0.5 * LOG2E)
                for mp in range(2):
                    lo = hl * HEAD_DIM + mp * DIFF_QK_DIM
                    keep = (lane_q >= lo) & (lane_q < lo + DIFF_QK_DIM)
                    qaug_ref[hl * 2 + mp] = jnp.where(keep, qs, jnp.where(own, 0.0, qa)).astype(BF16)
            else:
                qaug_ref[hl] = jnp.where(own, q * (HEAD_DIM ** -0.5 * LOG2E), qa_ref[hl].astype(F32)).astype(BF16)
        if mode == G_MOBA:
            blk_iota = _row_iota((LANES, tq))
            qblk = qpos // MOBA_BLOCK
            for hl in range(2):
                own = (lane_q // HEAD_DIM) == hl
                qg = jnp.where(own, q, 0.0).astype(BF16)
                g = _dot_nt(km_ref[...], qg)
                g = jnp.where(blk_iota < qblk, g, NEG_INF)
                sel = jnp.zeros((LANES, tq), F32)
                for _ in range(MOBA_TOPK):
                    mx = jnp.max(g, axis=0, keepdims=True)
                    first = jnp.min(jnp.where(g == mx, blk_iota, LANES), axis=0, keepdims=True)
                    hit = blk_iota == first
                    sel = jnp.where(hit, 1.0, sel)
                    g = jnp.where(hit, -jnp.inf, g)
                sel_ref[hl] = jnp.where(blk_iota < qblk, sel, 0.0)

    def online(s, vt_h, idx):
        m_prev = m_ref[idx]
        m_new = jnp.maximum(m_prev, jnp.max(s, axis=0, keepdims=True))
        p = jnp.exp2(s - m_new)
        alpha = jnp.exp2(m_prev - m_new)
        acc_ref[idx] = alpha * acc_ref[idx] + _dot(vt_h, p.astype(BF16))
        m_ref[idx] = m_new

    def step(diag):
        k = k_ref[...]
        kpos = kt * tk + _row_iota((tk, 1))
        items = []
        for hl in range(2):
            own_row = ((lane_row // HEAD_DIM) == hl).astype(F32).astype(BF16)
            vt_h = vt_ref[hl * HEAD_DIM:(hl + 1) * HEAD_DIM, :]
            if mode == G_SB:
                z = _dot_nt(k * own_row, qaug_ref[hl])
                lk = -_softplus(z)
                if diag:
                    strict = kpos < qpos
                    lk = jnp.where(strict, lk, 0.0)
                carry = carry_ref[hl]
                for hb in range(tk // MOBA_BLOCK - 1, -1, -1):
                    rows = slice(hb * MOBA_BLOCK, (hb + 1) * MOBA_BLOCK)
                    lk_h = lk[rows]
                    lk_hi = lk_h.astype(BF16)
                    lk_lo = (lk_h - lk_hi.astype(F32)).astype(BF16)
                    rest = _dot(u_ref[...], lk_hi) + _dot(u_ref[...], lk_lo) + carry
                    a = jnp.exp(z[rows] + lk_h + rest)
                    if diag:
                        a = jnp.where(strict[rows], a, 0.0)
                    acc_ref[hl] = acc_ref[hl] + _dot(vt_h[:, rows], a.astype(BF16))
                    carry = carry + jnp.sum(lk_h, axis=0, keepdims=True)
                carry_ref[hl] = carry
                continue
            vt_h = jnp.concatenate([vt_h, jnp.ones((ONES_ROWS, tk), BF16)], axis=0)
            kaug = k * own_row + ka_ref[hl]
            n_maps = 2 if mode == G_DIFF else 1
            for mp in range(n_maps):
                items.append((hl, hl * n_maps + mp, kaug, vt_h))

        def scores(item):
            hl, idx, kaug, _ = item
            s = _dot_nt(kaug, qaug_ref[idx])
            if mode == G_MOBA:
                parts = []
                for b in range(tk // MOBA_BLOCK):
                    blk = kt * (tk // MOBA_BLOCK) + b
                    rows = slice(b * MOBA_BLOCK, (b + 1) * MOBA_BLOCK)
                    allowed = sel_ref[hl, pl.ds(blk, 1), :] > 0.0
                    if diag:
                        allowed = allowed | (((qpos // MOBA_BLOCK) == blk) & (kpos[rows] <= qpos))
                    parts.append(jnp.where(allowed, s[rows], NEG_INF))
                s = parts[0] if len(parts) == 1 else jnp.concatenate(parts, axis=0)
            elif diag:
                s = jnp.where(kpos <= qpos, s, NEG_INF)
            return s

        if items:
            s_next = scores(items[0])
            for i, item in enumerate(items):
                s_cur = s_next
                if i + 1 < len(items):
                    s_next = scores(items[i + 1])
                online(s_cur, item[3], item[1])

    touches_diag = (kt * tk + tk - 1) > (qi * tq)

    def run():
        @pl.when(touches_diag)
        def _diag():
            step(True)

        @pl.when(jnp.logical_not(touches_diag))
        def _interior():
            step(False)

    if mode == G_SB:
        pl.when(jnp.max(carry_ref[...]) > SB_DEAD)(run)
    else:
        run()

    @pl.when(kj == last_k)
    def _finalize():
        if mode == G_SB:
            o = jnp.concatenate([acc_ref[0], acc_ref[1]], axis=0).T
        elif mode == G_DIFF:
            lp = lam_ref[...]
            lam_init = lp[4:5, 0:1]
            lam = (jnp.exp(jnp.sum(lp[0:1] * lp[1:2], axis=-1, keepdims=True))
                   - jnp.exp(jnp.sum(lp[2:3] * lp[3:4], axis=-1, keepdims=True)) + lam_init)
            outs = []
            for hl in range(2):
                a1, a2 = acc_ref[2 * hl], acc_ref[2 * hl + 1]
                o_h = (a1[0:HEAD_DIM] / a1[HEAD_DIM:HEAD_DIM + 1]
                       - lam * (a2[0:HEAD_DIM] / a2[HEAD_DIM:HEAD_DIM + 1]))
                ms = jnp.mean(o_h * o_h, axis=0, keepdims=True)
                outs.append(o_h * lax.rsqrt(ms + RMS_EPS))
            o = jnp.concatenate(outs, axis=0).T * sg_ref[...] * (1.0 - lam_init)
        else:
            a0, a1 = acc_ref[0], acc_ref[1]
            o = jnp.concatenate([a0[0:HEAD_DIM] / a0[HEAD_DIM:HEAD_DIM + 1],
                                 a1[0:HEAD_DIM] / a1[HEAD_DIM:HEAD_DIM + 1]], axis=0).T
        o_ref[...] = o.astype(o_ref.dtype)


def _lam_params(lq1, lk1, lq2, lk2, lam_init):
    rows = [jnp.pad(a.astype(F32), (0, LANES - a.shape[0])) for a in (lq1, lk1, lq2, lk2)]
    rows.append(jnp.full((LANES,), lam_init, F32))
    rows += [jnp.zeros((LANES,), F32)] * 3
    return jnp.stack(rows)


def _attn_tiles(mode, t):
    tq = min(512, t)
    tk = min(512 if mode == G_SB else 2048, t)
    return tq, tk


def prompt_attention(mode, q, k, vt, extras):
    t = q.shape[0]
    tq, tk = _attn_tiles(mode, t)
    nq = t // tq
    col = mode * 2

    def last_tile(qi):
        return (qi * tq + tq - 1) // tk

    qi_list, kj_list = [], []
    for qi in range(nq):
        for kj in range(last_tile(qi) + 1):
            qi_list.append(qi)
            kj_list.append(kj)
    qi_tab = jnp.asarray(np.array(qi_list, np.int32))
    kj_tab = jnp.asarray(np.array(kj_list, np.int32))

    def key_tile(s, qt, kt):
        return last_tile(qt[s]) - kt[s]

    const = lambda p, s, qt, kt: (0, 0)
    in_specs = [pl.BlockSpec((tq, PAIR_WIDTH), lambda p, s, qt, kt: (qt[s], col + p)),
                pl.BlockSpec((tk, PAIR_WIDTH), lambda p, s, qt, kt: (key_tile(s, qt, kt), col + p)),
                pl.BlockSpec((PAIR_WIDTH, tk), lambda p, s, qt, kt: (col + p, key_tile(s, qt, kt)))]
    aug_specs = [pl.BlockSpec((2, tq, LANES), lambda p, s, qt, kt: (p, qt[s], 0)),
                 pl.BlockSpec((2, tk, LANES), lambda p, s, qt, kt: (p, key_tile(s, qt, kt), 0))]
    n_state = 4 if mode == G_DIFF else 2
    qaug = pltpu.VMEM((n_state, tq, PAIR_WIDTH), BF16)
    stat = pltpu.VMEM((n_state, 1, tq), F32)
    acc = pltpu.VMEM((n_state, HEAD_DIM if mode == G_SB else HEAD_DIM + ONES_ROWS, tq), F32)
    if mode == G_MOBA:
        in_specs += aug_specs + [pl.BlockSpec((LANES, PAIR_WIDTH), lambda p, s, qt, kt: (0, p))]
        scratch = [qaug, stat, acc, pltpu.VMEM((2, LANES, tq), F32)]
    elif mode == G_FOX:
        in_specs += aug_specs
        scratch = [qaug, stat, acc]
    elif mode == G_SB:
        idx = jnp.arange(MOBA_BLOCK)
        extras = ((idx[None, :] > idx[:, None]).astype(BF16),)
        in_specs.append(pl.BlockSpec((MOBA_BLOCK, MOBA_BLOCK), const))
        scratch = [qaug, stat, acc]
    else:
        in_specs += aug_specs + [pl.BlockSpec((8, LANES), const), pl.BlockSpec((1, PAIR_WIDTH), const)]
        scratch = [qaug, stat, acc]
    grid_spec = pltpu.PrefetchScalarGridSpec(
        num_scalar_prefetch=2,
        grid=(2, len(qi_list)),
        in_specs=in_specs,
        out_specs=pl.BlockSpec((tq, PAIR_WIDTH), lambda p, s, qt, kt: (qt[s], p)),
        scratch_shapes=scratch,
    )
    return pl.pallas_call(
        functools.partial(_attn_kernel, mode, tq, tk),
        grid_spec=grid_spec,
        out_shape=jax.ShapeDtypeStruct((t, GROUP_WIDTH), BF16),
        compiler_params=_cparams(2),
        name="prompt_attn_%d" % mode,
    )(qi_tab, kj_tab, q, k, vt, *extras)


def _outproj_kernel(h_ref, o0_ref, o1_ref, o2_ref, o3_ref, w_ref, g_ref, out_ref):
    m = _dot(o0_ref[...], w_ref[0:GROUP_WIDTH, :])
    for gi, o_ref in enumerate((o1_ref, o2_ref, o3_ref), start=1):
        m = m + _dot(o_ref[...], w_ref[gi * GROUP_WIDTH:(gi + 1) * GROUP_WIDTH, :])
    out_ref[...] = h_ref[...] + _rms(m, g_ref[...])


def out_project(h, outs, w_out, g):
    rows, d = h.shape
    tm = min(rows, 1024)
    o_spec = pl.BlockSpec((tm, GROUP_WIDTH), lambda i: (i, 0))
    return pl.pallas_call(
        _outproj_kernel,
        grid=(rows // tm,),
        in_specs=[pl.BlockSpec((tm, d), lambda i: (i, 0)), o_spec, o_spec, o_spec, o_spec,
                  pl.BlockSpec((d, d), lambda i: (0, 0)), pl.BlockSpec((1, d), lambda i: (0, 0))],
        out_specs=pl.BlockSpec((tm, d), lambda i: (i, 0)),
        out_shape=jax.ShapeDtypeStruct((rows, d), F32),
        compiler_params=_cparams(1),
        name="out_project",
    )(h, *outs, w_out, g)


ROWS = 32
S_FOX, S_D1, S_D2, S_SB = 0, 1, 2, 3


def _sample_kernel(n_pages, page, dec_seq,
                   pt_ref, q_ref, kn_ref, vn_ref, lfn_ref, lfnt_ref, ck_ref, cv_ref, clt_ref,
                   u_ref, lam_ref, sg_ref, o_ref,
                   qs_ref, newk_ref, newv_ref, m_ref, l_ref, acc_ref, sbc_ref, fcar_ref, crow_ref,
                   bm_ref, bl_ref, bg_ref, bacc_ref):
    j = pl.program_id(1)
    n_blocks = (n_pages * page) // MOBA_BLOCK
    pages_per_block = MOBA_BLOCK // page
    past_len = n_pages * page
    pp = n_pages - j
    row = _row_iota((ROWS, 1))
    row_head = row // dec_seq
    qpos = past_len + (row % dec_seq)
    kpos = pp * page + _lane_iota((1, page))
    lane_g = _lane_iota((ROWS, GROUP_WIDTH))
    own_lanes = (lane_g // HEAD_DIM) == row_head

    @pl.when(j == 0)
    def _init():
        q = q_ref[...]
        for c in range(4):
            qc = q[:, c * GROUP_WIDTH:(c + 1) * GROUP_WIDTH]
            qt = jnp.concatenate([qc] * GROUP_HEADS, axis=0)
            if c == G_DIFF:
                sub = lane_g % HEAD_DIM
                qs_ref[3] = jnp.where(own_lanes & (sub < DIFF_QK_DIM), qt, 0.0).astype(BF16)
                qs_ref[4] = jnp.where(own_lanes & (sub >= DIFF_QK_DIM), qt, 0.0).astype(BF16)
            else:
                qs_ref[c] = jnp.where(own_lanes, qt * (HEAD_DIM ** -0.5), 0.0).astype(BF16)
        pad = jnp.zeros((page - dec_seq, kn_ref.shape[-1]), F32)
        newk_ref[...] = jnp.concatenate([kn_ref[...], pad], axis=0).astype(BF16)
        newv_ref[...] = jnp.concatenate([vn_ref[...], pad], axis=0).astype(BF16)
        m_ref[...] = jnp.full_like(m_ref, NEG_INF)
        l_ref[...] = jnp.zeros_like(l_ref)
        acc_ref[...] = jnp.zeros_like(acc_ref)
        sbc_ref[...] = jnp.zeros_like(sbc_ref)
        lfn = lfn_ref[...]
        tok = _row_iota(lfn.shape)
        cum = jnp.zeros_like(lfn)
        for i in range(dec_seq):
            cum = cum + jnp.where(tok >= i, lfn[i:i + 1, :], 0.0)
        cum_t = jnp.concatenate([cum] * GROUP_HEADS, axis=0)
        lane = _lane_iota(cum_t.shape)
        crow_ref[...] = jnp.sum(jnp.where(lane == row_head, cum_t, 0.0), axis=-1, keepdims=True)
        fcar_ref[...] = -jnp.sum(lfnt_ref[...], axis=-1, keepdims=True)

    def page_step(kb, vb, lft):
        causal = kpos <= qpos
        dist = (qpos - kpos).astype(F32)

        s_raw = _dot_nt(qs_ref[G_MOBA], kb[:, 0:GROUP_WIDTH])
        s = jnp.where(causal, s_raw - _select_by_index(row_head, MOBA_SLOPES) * dist, NEG_INF)
        blk = jnp.where(j == 0, n_blocks, pp // pages_per_block)
        fresh = (j == 0) | ((pp % pages_per_block) == (pages_per_block - 1))
        m_prev = jnp.where(fresh, NEG_INF, bm_ref[blk])
        l_prev = jnp.where(fresh, 0.0, bl_ref[blk])
        g_prev = jnp.where(fresh, 0.0, bg_ref[blk])
        a_prev = jnp.where(fresh, 0.0, bacc_ref[blk])
        m_new = jnp.maximum(m_prev, jnp.max(s, axis=-1, keepdims=True))
        p = jnp.exp(s - m_new)
        alpha = jnp.exp(m_prev - m_new)
        bm_ref[blk] = m_new
        bl_ref[blk] = alpha * l_prev + jnp.sum(p, axis=-1, keepdims=True)
        bg_ref[blk] = g_prev + jnp.sum(s_raw, axis=-1, keepdims=True)
        bacc_ref[blk] = alpha * a_prev + _dot(p.astype(BF16), vb[:, 0:GROUP_WIDTH])

        suffix = _dot_f32_lhs(lft, u_ref[...]) + fcar_ref[...]
        suffix_rows = jnp.concatenate(
            [jnp.broadcast_to(suffix[h:h + 1, :], (dec_seq, page)) for h in range(GROUP_HEADS)], axis=0)
        s = _dot_nt(qs_ref[G_FOX], kb[:, GROUP_WIDTH:2 * GROUP_WIDTH]) + crow_ref[...] + suffix_rows
        _online_update(s, causal, vb[:, GROUP_WIDTH:2 * GROUP_WIDTH], m_ref, l_ref, acc_ref, S_FOX, False)
        fcar_ref[...] = fcar_ref[...] + jnp.sum(lft, axis=-1, keepdims=True)

        z = _dot_nt(qs_ref[G_SB], kb[:, 2 * GROUP_WIDTH:3 * GROUP_WIDTH])
        strict = kpos < qpos
        lk = jnp.where(strict, -_softplus(z), 0.0)
        lk_hi = lk.astype(BF16)
        lk_lo = (lk - lk_hi.astype(F32)).astype(BF16)
        rest = _dot(lk_hi, u_ref[...]) + _dot(lk_lo, u_ref[...])
        a = jnp.where(strict, jnp.exp(z + lk + rest + sbc_ref[...]), 0.0)
        acc_ref[S_SB] = acc_ref[S_SB] + _dot(a.astype(BF16), vb[:, 2 * GROUP_WIDTH:3 * GROUP_WIDTH])
        sbc_ref[...] = sbc_ref[...] + jnp.sum(lk, axis=-1, keepdims=True)

        bias_d = -_select_by_index(row_head, DIFF_SLOPES) * dist
        kd = kb[:, 3 * GROUP_WIDTH:4 * GROUP_WIDTH]
        vd = vb[:, 3 * GROUP_WIDTH:4 * GROUP_WIDTH]
        for mp, slot in ((0, S_D1), (1, S_D2)):
            s = _dot_nt(qs_ref[3 + mp], kd) * (DIFF_QK_DIM ** -0.5) + bias_d
            _online_update(s, causal, vd, m_ref, l_ref, acc_ref, slot, False)

    @pl.when(j == 0)
    def _new_tokens():
        page_step(newk_ref[...], newv_ref[...], lfnt_ref[...])

    @pl.when(j > 0)
    def _past_page():
        page_step(ck_ref[...].astype(BF16), cv_ref[...].astype(BF16), clt_ref[...])

    @pl.when(j == n_pages)
    def _finalize():
        def gather_heads(o):
            o = jnp.where(own_lanes, o, 0.0)
            out = o[0:dec_seq]
            for h in range(1, GROUP_HEADS):
                out = out + o[h * dec_seq:(h + 1) * dec_seq]
            return out

        gates = bg_ref[0:n_blocks]
        bidx = lax.broadcasted_iota(jnp.int32, gates.shape, 0)
        sel = jnp.zeros(gates.shape, F32)
        for _ in range(min(MOBA_TOPK, n_blocks)):
            mx = jnp.max(gates, axis=0, keepdims=True)
            first = jnp.min(jnp.where(gates == mx, bidx, n_blocks), axis=0, keepdims=True)
            hit = bidx == first
            sel = jnp.where(hit, 1.0, sel)
            gates = jnp.where(hit, -jnp.inf, gates)
        picked = sel > 0.0
        bm = bm_ref[0:n_blocks]
        m_own = bm_ref[n_blocks]
        m_fin = jnp.maximum(m_own, jnp.max(jnp.where(picked, bm, NEG_INF), axis=0))
        w = jnp.where(picked, jnp.exp(bm - m_fin), 0.0)
        w_own = jnp.exp(m_own - m_fin)
        den = w_own * bl_ref[n_blocks] + jnp.sum(w * bl_ref[0:n_blocks], axis=0)
        num = w_own * bacc_ref[n_blocks] + jnp.sum(w * bacc_ref[0:n_blocks], axis=0)
        o_ref[:, 0:GROUP_WIDTH] = gather_heads(num / den)

        o_ref[:, GROUP_WIDTH:2 * GROUP_WIDTH] = gather_heads(acc_ref[S_FOX] / l_ref[S_FOX])
        o_ref[:, 2 * GROUP_WIDTH:3 * GROUP_WIDTH] = gather_heads(acc_ref[S_SB])

        lp = lam_ref[...]
        lam_init = lp[4:5, 0:1]
        lam = (jnp.exp(jnp.sum(lp[0:1] * lp[1:2], axis=-1, keepdims=True))
               - jnp.exp(jnp.sum(lp[2:3] * lp[3:4], axis=-1, keepdims=True)) + lam_init)
        o_d = acc_ref[S_D1] / l_ref[S_D1] - lam * (acc_ref[S_D2] / l_ref[S_D2])
        ms = jnp.sum(jnp.where(own_lanes, o_d * o_d, 0.0), axis=-1, keepdims=True) * (1.0 / HEAD_DIM)
        o_d = o_d * lax.rsqrt(ms + RMS_EPS) * sg_ref[...] * (1.0 - lam_init)
        o_ref[:, 3 * GROUP_WIDTH:4 * GROUP_WIDTH] = gather_heads(o_d)


def sample_attention(layer, q, k_new, v_new, lf_new, cache_k, cache_v, cache_lft, page_table, lam_p, subln):
    b, dec_seq, d = q.shape
    n_pages = page_table.shape[1]
    page = cache_k.shape[2]
    n_blocks = (n_pages * page) // MOBA_BLOCK
    assert ROWS == GROUP_HEADS * dec_seq and page == LANES and MOBA_BLOCK % page == 0
    lf_new_t = jnp.pad(jnp.swapaxes(lf_new[:, :, 0:GROUP_HEADS], 1, 2),
                       ((0, 0), (0, 8 - GROUP_HEADS), (0, page - dec_seq)))
    idx = jnp.arange(page)
    u = (idx[:, None] > idx[None, :]).astype(BF16)
    subln4 = jnp.tile(subln, (1, GROUP_HEADS // 2))

    def page_idx(bi, j, pt):
        return pt[bi, n_pages - jnp.maximum(j, 1)]

    per_batch = lambda bi, j, pt: (bi, 0, 0)
    const2 = lambda bi, j, pt: (0, 0)
    grid_spec = pltpu.PrefetchScalarGridSpec(
        num_scalar_prefetch=1,
        grid=(b, n_pages + 1),
        in_specs=[
            pl.BlockSpec((None, dec_seq, d), per_batch),
            pl.BlockSpec((None, dec_seq, d), per_batch),
            pl.BlockSpec((None, dec_seq, d), per_batch),
            pl.BlockSpec((None, dec_seq, LANES), per_batch),
            pl.BlockSpec((None, 8, page), per_batch),
            pl.BlockSpec((None, None, page, d), lambda bi, j, pt: (layer, page_idx(bi, j, pt), 0, 0)),
            pl.BlockSpec((None, None, page, d), lambda bi, j, pt: (layer, page_idx(bi, j, pt), 0, 0)),
            pl.BlockSpec((None, None, 8, page), lambda bi, j, pt: (layer, page_idx(bi, j, pt), 0, 0)),
            pl.BlockSpec((page, page), const2),
            pl.BlockSpec((8, LANES), const2),
            pl.BlockSpec((1, GROUP_WIDTH), const2),
        ],
        out_specs=pl.BlockSpec((None, dec_seq, d), per_batch),
        scratch_shapes=[
            pltpu.VMEM((5, ROWS, GROUP_WIDTH), BF16),
            pltpu.VMEM((page, d), BF16), pltpu.VMEM((page, d), BF16),
            pltpu.VMEM((3, ROWS, 1), F32), pltpu.VMEM((3, ROWS, 1), F32),
            pltpu.VMEM((4, ROWS, GROUP_WIDTH), F32),
            pltpu.VMEM((ROWS, 1), F32), pltpu.VMEM((8, 1), F32), pltpu.VMEM((ROWS, 1), F32),
            pltpu.VMEM((n_blocks + 1, ROWS, 1), F32), pltpu.VMEM((n_blocks + 1, ROWS, 1), F32),
            pltpu.VMEM((n_blocks + 1, ROWS, 1), F32), pltpu.VMEM((n_blocks + 1, ROWS, GROUP_WIDTH), F32),
        ],
    )
    return pl.pallas_call(
        functools.partial(_sample_kernel, n_pages, page, dec_seq),
        grid_spec=grid_spec,
        out_shape=jax.ShapeDtypeStruct((b, dec_seq, d), F32),
        compiler_params=_cparams(2),
        name="sample_attn",
    )(page_table, q, k_new, v_new, lf_new, lf_new_t, cache_k, cache_v, cache_lft, u, lam_p, subln4)


def kernel(x_prompt, x_sample, cache_k, cache_v, cache_logf, page_table, norm_g, w_in, b_f, w_out,
           diff_lq1, diff_lk1, diff_lq2, diff_lk2, diff_subln_g, ffn_w_gate, ffn_w_up, ffn_w_down):
    depth = norm_g.shape[0]
    bp, seq, d = x_prompt.shape
    bs, dec_seq, _ = x_sample.shape
    n_phys, page = cache_k.shape[1], cache_k.shape[2]
    assert bp == 1 and d == D_MODEL
    mix = N_HEADS * HEAD_DIM

    ck = cache_k.reshape(depth, n_phys, page, mix)
    cv = cache_v.reshape(depth, n_phys, page, mix)
    clt = jnp.pad(jnp.swapaxes(cache_logf, 2, 3), ((0, 0), (0, 0), (0, 8 - GROUP_HEADS), (0, 0)))

    moba_qa, moba_ka = alibi_aug(seq, MOBA_SLOPES)
    diff_qa, diff_ka = alibi_aug(seq, DIFF_SLOPES)

    hp = x_prompt.reshape(seq, d)
    hs = x_sample.reshape(bs * dec_seq, d)
    pk, pv, pf, sk, sv, sf = [], [], [], [], [], []
    for l in range(depth):
        lam_init = 0.8 - 0.6 * math.exp(-0.3 * l)
        g = norm_g[l].reshape(6, 1, d)
        wg = ffn_w_gate[l].astype(BF16)
        wu = ffn_w_up[l].astype(BF16)
        wd = ffn_w_down[l].astype(BF16)
        w_qkv = w_in[l, :, 0:3 * mix].astype(BF16)
        w_f = jnp.pad(w_in[l, :, 3 * mix:], ((0, 0), (0, LANES - GROUP_HEADS))).astype(BF16)
        bias_f = jnp.pad(b_f[l].astype(F32), (0, LANES - GROUP_HEADS)).reshape(1, LANES)
        wo = w_out[l].astype(BF16)
        lam_p = _lam_params(diff_lq1[l], diff_lk1[l], diff_lq2[l], diff_lk2[l], lam_init)
        subln = jnp.tile(diff_subln_g[l].astype(F32).reshape(1, HEAD_DIM), (1, 2))

        hp = ffn_block(hp, g[0], g[1], wg[0], wu[0], wd[0])
        q, kf, vf, kb, vt, lf, kmean = project(hp, g[2], w_qkv, w_f, bias_f, True)
        fox_qa, fox_ka = fox_aug(*cumsum_logf(lf))
        n_blk = seq // MOBA_BLOCK
        km = jnp.pad(kmean.reshape(n_blk, GROUP_WIDTH), ((0, LANES - n_blk), (0, 0))).astype(BF16)
        outs = [prompt_attention(G_MOBA, q, kb, vt, (moba_qa, moba_ka, km)),
                prompt_attention(G_FOX, q, kb, vt, (fox_qa, fox_ka)),
                prompt_attention(G_SB, q, kb, vt, ()),
                prompt_attention(G_DIFF, q, kb, vt, (diff_qa, diff_ka, lam_p, subln))]
        hp = out_project(hp, outs, wo, g[3])
        hp = ffn_block(hp, g[4], g[5], wg[1], wu[1], wd[1])
        pk.append(kf.reshape(bp, seq, N_HEADS, HEAD_DIM))
        pv.append(vf.reshape(bp, seq, N_HEADS, HEAD_DIM))
        pf.append(lf[:, 0:GROUP_HEADS].reshape(bp, seq, GROUP_HEADS))

        hs = ffn_block(hs, g[0], g[1], wg[0], wu[0], wd[0])
        q, kf, vf, kb, vb, lf = project(hs, g[2], w_qkv, w_f, bias_f, False)
        o = sample_attention(l, q.reshape(bs, dec_seq, d), kf.reshape(bs, dec_seq, d),
                             vf.reshape(bs, dec_seq, d), lf.reshape(bs, dec_seq, LANES),
                             ck, cv, clt, page_table, lam_p, subln)
        o = o.reshape(bs * dec_seq, d).astype(BF16)
        outs = [o[:, i * GROUP_WIDTH:(i + 1) * GROUP_WIDTH] for i in range(4)]
        hs = out_project(hs, outs, wo, g[3])
        hs = ffn_block(hs, g[4], g[5], wg[1], wu[1], wd[1])
        sk.append(kf.reshape(bs, dec_seq, N_HEADS, HEAD_DIM))
        sv.append(vf.reshape(bs, dec_seq, N_HEADS, HEAD_DIM))
        sf.append(lf[:, 0:GROUP_HEADS].reshape(bs, dec_seq, GROUP_HEADS))

    return (hp.reshape(bp, seq, d), hs.reshape(bs, dec_seq, d),
            jnp.stack(pk), jnp.stack(pv), jnp.stack(pf), jnp.stack(sk), jnp.stack(sv), jnp.stack(sf))
```

```python
import functools
import math

import numpy as np
import jax
import jax.numpy as jnp
from jax import lax
from jax.experimental import pallas as pl
from jax.experimental.pallas import tpu as pltpu

F32 = jnp.float32
BF16 = jnp.bfloat16

D_MODEL = 1024
N_HEADS = 16
HEAD_DIM = 64
GROUP_HEADS = 4
GROUP_WIDTH = GROUP_HEADS * HEAD_DIM
PAIR_WIDTH = 2 * HEAD_DIM
DIFF_QK_DIM = HEAD_DIM // 2
MOBA_BLOCK = 256
MOBA_TOPK = 3
RMS_EPS = 1e-6
NEG_INF = -1e30
LANES = 128
G_MOBA, G_FOX, G_SB, G_DIFF = 0, 1, 2, 3
MOBA_SLOPES = tuple(2.0 ** -(2 * h + 1) for h in range(GROUP_HEADS))
DIFF_SLOPES = tuple(2.0 ** -(2 * h + 2) for h in range(GROUP_HEADS))
VMEM_LIMIT = 56 * 1024 * 1024
LOG2E = 1.4426950408889634
ONES_ROWS = 16
SB_DEAD = -104.0


def _cparams(n_axes):
    return pltpu.CompilerParams(dimension_semantics=("arbitrary",) * n_axes,
                                vmem_limit_bytes=VMEM_LIMIT)


def _rms(x, g):
    return x * lax.rsqrt(jnp.mean(x * x, axis=-1, keepdims=True) + RMS_EPS) * g


def _dot(a, b):
    return jnp.dot(a, b, preferred_element_type=F32)


def _dot_nt(a, b):
    return lax.dot_general(a, b, (((1,), (1,)), ((), ())), preferred_element_type=F32)


def _split3(a):
    a1 = a.astype(BF16)
    r1 = a - a1.astype(F32)
    a2 = r1.astype(BF16)
    a3 = (r1 - a2.astype(F32)).astype(BF16)
    return a1, a2, a3


def _dot_f32_lhs(a, b01):
    a1, a2, a3 = _split3(a)
    return _dot(a1, b01) + _dot(a2, b01) + _dot(a3, b01)


def _dot_f32_rhs(a01, b):
    b1, b2, b3 = _split3(b)
    return _dot(a01, b1) + _dot(a01, b2) + _dot(a01, b3)


def _log_sigmoid(x):
    return jnp.minimum(x, 0.0) - jnp.log1p(jnp.exp(-jnp.abs(x)))


def _softplus(x):
    return jnp.maximum(x, 0.0) + jnp.log1p(jnp.exp(-jnp.abs(x)))


def _lane_iota(shape):
    return lax.broadcasted_iota(jnp.int32, shape, len(shape) - 1)


def _row_iota(shape):
    return lax.broadcasted_iota(jnp.int32, shape, len(shape) - 2)


def _select_by_index(idx, values):
    out = jnp.full(jnp.shape(idx), values[-1], F32)
    for i in range(len(values) - 2, -1, -1):
        out = jnp.where(idx == i, values[i], out)
    return out


def _ffn_kernel(x_ref, gpre_ref, gpost_ref, wg_ref, wu_ref, wd_ref, o_ref, xn_ref, acc_ref):
    j = pl.program_id(1)

    @pl.when(j == 0)
    def _():
        xn_ref[...] = _rms(x_ref[...], gpre_ref[...]).astype(BF16)
        acc_ref[...] = jnp.zeros_like(acc_ref)

    xn = xn_ref[...]
    gate = _dot(xn, wg_ref[...])
    up = _dot(xn, wu_ref[...])
    mid = gate * jax.nn.sigmoid(gate) * up
    acc_ref[...] += _dot(mid.astype(BF16), wd_ref[...])

    @pl.when(j == pl.num_programs(1) - 1)
    def _():
        o_ref[...] = x_ref[...] + 0.5 * _rms(acc_ref[...], gpost_ref[...])


def _ffn_tiles(rows, d_ff):
    tm = min(rows, 512)
    tf = d_ff
    for cand in (1408, 1024, 512, 256, 128):
        if d_ff % cand == 0:
            tf = cand
            break
    return tm, tf


def ffn_block(x, g_pre, g_post, w_gate, w_up, w_down):
    rows, d = x.shape
    d_ff = w_gate.shape[1]
    tm, tf = _ffn_tiles(rows, d_ff)
    return pl.pallas_call(
        _ffn_kernel,
        grid=(rows // tm, d_ff // tf),
        in_specs=[
            pl.BlockSpec((tm, d), lambda i, j: (i, 0)),
            pl.BlockSpec((1, d), lambda i, j: (0, 0)),
            pl.BlockSpec((1, d), lambda i, j: (0, 0)),
            pl.BlockSpec((d, tf), lambda i, j: (0, j)),
            pl.BlockSpec((d, tf), lambda i, j: (0, j)),
            pl.BlockSpec((tf, d), lambda i, j: (j, 0)),
        ],
        out_specs=pl.BlockSpec((tm, d), lambda i, j: (i, 0)),
        out_shape=jax.ShapeDtypeStruct((rows, d), F32),
        scratch_shapes=[pltpu.VMEM((tm, d), BF16), pltpu.VMEM((tm, d), F32)],
        compiler_params=_cparams(2),
        name="ffn_block",
    )(x, g_pre, g_post, w_gate, w_up, w_down)


def _proj_kernel(with_kmean, h_ref, g_ref, w_ref, wf_ref, bf_ref, *out_refs):
    if with_kmean:
        q_ref, kf_ref, vf_ref, kb_ref, vb_ref, lf_ref, km_ref = out_refs
    else:
        q_ref, kf_ref, vf_ref, kb_ref, vb_ref, lf_ref = out_refs
    hn = _rms(h_ref[...], g_ref[...]).astype(BF16)
    q = _dot(hn, w_ref[:, 0:D_MODEL])
    q_ref[...] = q.astype(q_ref.dtype)
    k = _dot(hn, w_ref[:, D_MODEL:2 * D_MODEL])
    kf_ref[...] = k
    kb_ref[...] = k.astype(BF16)
    v = _dot(hn, w_ref[:, 2 * D_MODEL:3 * D_MODEL])
    vf_ref[...] = v
    vb_ref[...] = (v.T if with_kmean else v).astype(BF16)
    lf_ref[...] = _log_sigmoid(_dot(hn, wf_ref[...]) + bf_ref[...])
    if with_kmean:
        for i in range(k.shape[0] // MOBA_BLOCK):
            blk = k[i * MOBA_BLOCK:(i + 1) * MOBA_BLOCK, 0:GROUP_WIDTH]
            km_ref[i] = jnp.mean(blk, axis=0, keepdims=True)


def project(h, g, w_qkv, w_f, b_f, with_kmean):
    rows, d = h.shape
    tm = min(rows, 512)
    row_spec = pl.BlockSpec((tm, d), lambda i: (i, 0))
    q_dtype = BF16 if with_kmean else F32
    out_shapes = [jax.ShapeDtypeStruct((rows, d), q_dtype),
                  jax.ShapeDtypeStruct((rows, d), F32), jax.ShapeDtypeStruct((rows, d), F32),
                  jax.ShapeDtypeStruct((rows, d), BF16),
                  jax.ShapeDtypeStruct((d, rows) if with_kmean else (rows, d), BF16),
                  jax.ShapeDtypeStruct((rows, LANES), F32)]
    v_spec = pl.BlockSpec((d, tm), lambda i: (0, i)) if with_kmean else row_spec
    out_specs = [row_spec] * 4 + [v_spec, pl.BlockSpec((tm, LANES), lambda i: (i, 0))]
    if with_kmean:
        out_shapes.append(jax.ShapeDtypeStruct((rows // MOBA_BLOCK, 1, GROUP_WIDTH), F32))
        out_specs.append(pl.BlockSpec((tm // MOBA_BLOCK, 1, GROUP_WIDTH), lambda i: (i, 0, 0)))
    return pl.pallas_call(
        functools.partial(_proj_kernel, with_kmean),
        grid=(rows // tm,),
        in_specs=[
            row_spec,
            pl.BlockSpec((1, d), lambda i: (0, 0)),
            pl.BlockSpec((d, 3 * d), lambda i: (0, 0)),
            pl.BlockSpec((d, LANES), lambda i: (0, 0)),
            pl.BlockSpec((1, LANES), lambda i: (0, 0)),
        ],
        out_specs=out_specs,
        out_shape=out_shapes,
        compiler_params=_cparams(1),
        name="project",
    )(h, g, w_qkv, w_f, b_f)


def _cumsum_kernel(lf_ref, tri_ref, p1_ref, p2_ref, p3_ref):
    n_chunks = lf_ref.shape[0] // LANES
    tri = tri_ref[...]

    def body(i, carry):
        rows = pl.ds(pl.multiple_of(i * LANES, LANES), LANES)
        c = _dot_f32_rhs(tri, lf_ref[rows, :]) + carry
        p1, p2, p3 = _split3(c * LOG2E)
        p1_ref[rows, :] = p1
        p2_ref[rows, :] = p2
        p3_ref[rows, :] = p3
        return c[LANES - 1:LANES, :]

    lax.fori_loop(0, n_chunks, body, jnp.zeros((1, LANES), F32))


def cumsum_logf(lf):
    rows = lf.shape[0]
    idx = jnp.arange(LANES)
    tri = (idx[None, :] <= idx[:, None]).astype(BF16)
    piece = jax.ShapeDtypeStruct((rows, LANES), BF16)
    return pl.pallas_call(
        _cumsum_kernel,
        out_shape=[piece, piece, piece],
        compiler_params=pltpu.CompilerParams(vmem_limit_bytes=VMEM_LIMIT),
        name="cumsum_logf",
    )(lf, tri)


def _split3_const(x):
    x = np.float32(x)
    a1 = np.float32(np.asarray(x, dtype=BF16))
    a2 = np.float32(np.asarray(x - a1, dtype=BF16))
    a3 = np.float32(np.asarray(x - a1 - a2, dtype=BF16))
    return float(a1), float(a2), float(a3)


def _aug_base(h):
    return HEAD_DIM if h % 2 == 0 else 0


def _place_lanes(columns, base):
    cols = jnp.stack(columns, axis=1)
    return jnp.pad(cols, ((0, 0), (base, LANES - base - len(columns))))


def alibi_aug(t, slopes):
    pos = jnp.arange(t, dtype=jnp.int32)
    hi = ((pos // LANES) * LANES).astype(F32)
    lo = (pos % LANES).astype(F32)
    qa_all, ka_all = [], []
    for h, slope in enumerate(slopes):
        c = [jnp.full((t,), ci, F32) for ci in _split3_const(LOG2E * slope)]
        qa_all.append(_place_lanes(c + c + [-hi] * 3 + [-lo] * 3, _aug_base(h)))
        ka_all.append(_place_lanes([hi] * 3 + [lo] * 3 + c + c, _aug_base(h)))
    return jnp.stack(qa_all).astype(BF16), jnp.stack(ka_all).astype(BF16)


def fox_aug(p1, p2, p3):
    t = p1.shape[0]
    one = jnp.ones((t,), BF16)
    qa_all, ka_all = [], []
    for h in range(GROUP_HEADS):
        pieces = [p[:, h] for p in (p1, p2, p3)]
        qa_all.append(_place_lanes(pieces + [-one] * 3, _aug_base(h)))
        ka_all.append(_place_lanes([one] * 3 + pieces, _aug_base(h)))
    return jnp.stack(qa_all), jnp.stack(ka_all)


def _attn_kernel(mode, tq, tk, qi_tab, kj_tab, *refs):
    if mode == G_MOBA:
        q_ref, k_ref, vt_ref, qa_ref, ka_ref, km_ref, o_ref, qaug_ref, m_ref, acc_ref, sel_ref = refs
    elif mode == G_FOX:
        q_ref, k_ref, vt_ref, qa_ref, ka_ref, o_ref, qaug_ref, m_ref, acc_ref = refs
    elif mode == G_SB:
        q_ref, k_ref, vt_ref, u_ref, o_ref, qaug_ref, carry_ref, acc_ref = refs
    else:
        q_ref, k_ref, vt_ref, qa_ref, ka_ref, lam_ref, sg_ref, o_ref, qaug_ref, m_ref, acc_ref = refs

    step_id = pl.program_id(1)
    qi = qi_tab[step_id]
    kj = kj_tab[step_id]
    last_k = (qi * tq + tq - 1) // tk
    kt = last_k - kj
    lane_q = _lane_iota((tq, PAIR_WIDTH))
    lane_row = _lane_iota((1, PAIR_WIDTH))
    qpos = qi * tq + _lane_iota((1, tq))

    @pl.when(kj == 0)
    def _init():
        acc_ref[...] = jnp.zeros_like(acc_ref)
        if mode == G_SB:
            carry_ref[...] = jnp.zeros_like(carry_ref)
        else:
            m_ref[...] = jnp.full_like(m_ref, NEG_INF)
        q = q_ref[...].astype(F32)
        for hl in range(2):
            own = (lane_q // HEAD_DIM) == hl
            if mode == G_SB:
                qaug_ref[hl] = jnp.where(own, q * (HEAD_DIM ** -0.5), 0.0).astype(BF16)
            elif mode == G_DIFF:
                qa = qa_ref[hl].astype(F32)
                qs = q * (DIFF_QK_DIM ** -0.5 * LOG2E)
                for mp in range(2):
                    lo = hl * HEAD_DIM + mp * DIFF_QK_DIM
                    keep = (lane_q >= lo) & (lane_q < lo + DIFF_QK_DIM)
                    qaug_ref[hl * 2 + mp] = jnp.where(keep, qs, jnp.where(own, 0.0, qa)).astype(BF16)
            else:
                qaug_ref[hl] = jnp.where(own, q * (HEAD_DIM ** -0.5 * LOG2E), qa_ref[hl].astype(F32)).astype(BF16)
        if mode == G_MOBA:
            blk_iota = _row_iota((LANES, tq))
            qblk = qpos // MOBA_BLOCK
            for hl in range(2):
                own = (lane_q // HEAD_DIM) == hl
                qg = jnp.where(own, q, 0.0).astype(BF16)
                g = _dot_nt(km_ref[...], qg)
                g = jnp.where(blk_iota < qblk, g, NEG_INF)
                sel = jnp.zeros((LANES, tq), F32)
                for _ in range(MOBA_TOPK):
                    mx = jnp.max(g, axis=0, keepdims=True)
                    first = jnp.min(jnp.where(g == mx, blk_iota, LANES), axis=0, keepdims=True)
                    hit = blk_iota == first
                    sel = jnp.where(hit, 1.0, sel)
                    g = jnp.where(hit, -jnp.inf, g)
                sel_ref[hl] = jnp.where(blk_iota < qblk, sel, 0.0)

    def online(s, vt_h, idx):
        m_prev = m_ref[idx]
        m_new = jnp.maximum(m_prev, jnp.max(s, axis=0, keepdims=True))
        p = jnp.exp2(s - m_new)
        alpha = jnp.exp2(m_prev - m_new)
        acc_ref[idx] = alpha * acc_ref[idx] + _dot(vt_h, p.astype(BF16))
        m_ref[idx] = m_new

    def step(diag):
        k = k_ref[...]
        kpos = kt * tk + _row_iota((tk, 1))
        items = []
        for hl in range(2):
            own_row = ((lane_row // HEAD_DIM) == hl).astype(F32).astype(BF16)
            vt_h = vt_ref[hl * HEAD_DIM:(hl + 1) * HEAD_DIM, :]
            if mode == G_SB:
                z = _dot_nt(k * own_row, qaug_ref[hl])
                lk = -_softplus(z)
                if diag:
                    strict = kpos < qpos
                    lk = jnp.where(strict, lk, 0.0)
                carry = carry_ref[hl]
                for hb in range(tk // MOBA_BLOCK - 1, -1, -1):
                    rows = slice(hb * MOBA_BLOCK, (hb + 1) * MOBA_BLOCK)
                    lk_h = lk[rows]
                    lk_hi = lk_h.astype(BF16)
                    lk_lo = (lk_h - lk_hi.astype(F32)).astype(BF16)
                    rest = _dot(u_ref[...], lk_hi) + _dot(u_ref[...], lk_lo) + carry
                    a = jnp.exp(z[rows] + lk_h + rest)
                    if diag:
                        a = jnp.where(strict[rows], a, 0.0)
                    acc_ref[hl] = acc_ref[hl] + _dot(vt_h[:, rows], a.astype(BF16))
                    carry = carry + jnp.sum(lk_h, axis=0, keepdims=True)
                carry_ref[hl] = carry
                continue
            vt_h = jnp.concatenate([vt_h, jnp.ones((ONES_ROWS, tk), BF16)], axis=0)
            kaug = k * own_row + ka_ref[hl]
            n_maps = 2 if mode == G_DIFF else 1
            for mp in range(n_maps):
                items.append((hl, hl * n_maps + mp, kaug, vt_h))

        def scores(item):
            hl, idx, kaug, _ = item
            s = _dot_nt(kaug, qaug_ref[idx])
            if mode == G_MOBA:
                parts = []
                for b in range(tk // MOBA_BLOCK):
                    blk = kt * (tk // MOBA_BLOCK) + b
                    rows = slice(b * MOBA_BLOCK, (b + 1) * MOBA_BLOCK)
                    allowed = sel_ref[hl, pl.ds(blk, 1), :] > 0.0
                    if diag:
                        allowed = allowed | (((qpos // MOBA_BLOCK) == blk) & (kpos[rows] <= qpos))
                    parts.append(jnp.where(allowed, s[rows], NEG_INF))
                s = parts[0] if len(parts) == 1 else jnp.concatenate(parts, axis=0)
            elif diag:
                s = jnp.where(kpos <= qpos, s, NEG_INF)
            return s

        if items:
            s_next = scores(items[0])
            for i, item in enumerate(items):
                s_cur = s_next
                if i + 1 < len(items):
                    s_next = scores(items[i + 1])
                online(s_cur, item[3], item[1])

    touches_diag = (kt * tk + tk - 1) > (qi * tq)

    def run():
        @pl.when(touches_diag)
        def _diag():
            step(True)

        @pl.when(jnp.logical_not(touches_diag))
        def _interior():
            step(False)

    if mode == G_SB:
        pl.when(jnp.max(carry_ref[...]) > SB_DEAD)(run)
    else:
        run()

    @pl.when(kj == last_k)
    def _finalize():
        if mode == G_SB:
            o = jnp.concatenate([acc_ref[0], acc_ref[1]], axis=0).T
        elif mode == G_DIFF:
            lp = lam_ref[...]
            lam_init = lp[4:5, 0:1]
            lam = (jnp.exp(jnp.sum(lp[0:1] * lp[1:2], axis=-1, keepdims=True))
                   - jnp.exp(jnp.sum(lp[2:3] * lp[3:4], axis=-1, keepdims=True)) + lam_init)
            outs = []
            for hl in range(2):
                a1, a2 = acc_ref[2 * hl], acc_ref[2 * hl + 1]
                o_h = (a1[0:HEAD_DIM] / a1[HEAD_DIM:HEAD_DIM + 1]
                       - lam * (a2[0:HEAD_DIM] / a2[HEAD_DIM:HEAD_DIM + 1]))
                ms = jnp.mean(o_h * o_h, axis=0, keepdims=True)
                outs.append(o_h * lax.rsqrt(ms + RMS_EPS))
            o = jnp.concatenate(outs, axis=0).T * sg_ref[...] * (1.0 - lam_init)
        else:
            a0, a1 = acc_ref[0], acc_ref[1]
            o = jnp.concatenate([a0[0:HEAD_DIM] / a0[HEAD_DIM:HEAD_DIM + 1],
                                 a1[0:HEAD_DIM] / a1[HEAD_DIM:HEAD_DIM + 1]], axis=0).T
        o_ref[...] = o.astype(o_ref.dtype)


def _lam_params(lq1, lk1, lq2, lk2, lam_init):
    rows = [jnp.pad(a.astype(F32), (0, LANES - a.shape[0])) for a in (lq1, lk1, lq2, lk2)]
    rows.append(jnp.full((LANES,), lam_init, F32))
    rows += [jnp.zeros((LANES,), F32)] * 3
    return jnp.stack(rows)


def _attn_tiles(mode, t):
    tq = min(512, t)
    tk = min(512 if mode == G_SB else 2048, t)
    return tq, tk


def prompt_attention(mode, q, k, vt, extras):
    t = q.shape[0]
    tq, tk = _attn_tiles(mode, t)
    nq = t // tq
    col = mode * 2

    def last_tile(qi):
        return (qi * tq + tq - 1) // tk

    qi_list, kj_list = [], []
    for qi in range(nq):
        for kj in range(last_tile(qi) + 1):
            qi_list.append(qi)
            kj_list.append(kj)
    qi_tab = jnp.asarray(np.array(qi_list, np.int32))
    kj_tab = jnp.asarray(np.array(kj_list, np.int32))

    def key_tile(s, qt, kt):
        return last_tile(qt[s]) - kt[s]

    const = lambda p, s, qt, kt: (0, 0)
    in_specs = [pl.BlockSpec((tq, PAIR_WIDTH), lambda p, s, qt, kt: (qt[s], col + p)),
                pl.BlockSpec((tk, PAIR_WIDTH), lambda p, s, qt, kt: (key_tile(s, qt, kt), col + p)),
                pl.BlockSpec((PAIR_WIDTH, tk), lambda p, s, qt, kt: (col + p, key_tile(s, qt, kt)))]
    aug_specs = [pl.BlockSpec((2, tq, LANES), lambda p, s, qt, kt: (p, qt[s], 0)),
                 pl.BlockSpec((2, tk, LANES), lambda p, s, qt, kt: (p, key_tile(s, qt, kt), 0))]
    n_state = 4 if mode == G_DIFF else 2
    qaug = pltpu.VMEM((n_state, tq, PAIR_WIDTH), BF16)
    stat = pltpu.VMEM((n_state, 1, tq), F32)
    acc = pltpu.VMEM((n_state, HEAD_DIM if mode == G_SB else HEAD_DIM + ONES_ROWS, tq), F32)
    if mode == G_MOBA:
        in_specs += aug_specs + [pl.BlockSpec((LANES, PAIR_WIDTH), lambda p, s, qt, kt: (0, p))]
        scratch = [qaug, stat, acc, pltpu.VMEM((2, LANES, tq), F32)]
    elif mode == G_FOX:
        in_specs += aug_specs
        scratch = [qaug, stat, acc]
    elif mode == G_SB:
        idx = jnp.arange(MOBA_BLOCK)
        extras = ((idx[None, :] > idx[:, None]).astype(BF16),)
        in_specs.append(pl.BlockSpec((MOBA_BLOCK, MOBA_BLOCK), const))
        scratch = [qaug, stat, acc]
    else:
        in_specs += aug_specs + [pl.BlockSpec((8, LANES), const), pl.BlockSpec((1, PAIR_WIDTH), const)]
        scratch = [qaug, stat, acc]
    grid_spec = pltpu.PrefetchScalarGridSpec(
        num_scalar_prefetch=2,
        grid=(2, len(qi_list)),
        in_specs=in_specs,
        out_specs=pl.BlockSpec((tq, PAIR_WIDTH), lambda p, s, qt, kt: (qt[s], p)),
        scratch_shapes=scratch,
    )
    return pl.pallas_call(
        functools.partial(_attn_kernel, mode, tq, tk),
        grid_spec=grid_spec,
        out_shape=jax.ShapeDtypeStruct((t, GROUP_WIDTH), BF16),
        compiler_params=_cparams(2),
        name="prompt_attn_%d" % mode,
    )(qi_tab, kj_tab, q, k, vt, *extras)


def _outproj_kernel(h_ref, o0_ref, o1_ref, o2_ref, o3_ref, w_ref, g_ref, out_ref):
    m = _dot(o0_ref[...], w_ref[0:GROUP_WIDTH, :])
    for gi, o_ref in enumerate((o1_ref, o2_ref, o3_ref), start=1):
        m = m + _dot(o_ref[...], w_ref[gi * GROUP_WIDTH:(gi + 1) * GROUP_WIDTH, :])
    out_ref[...] = h_ref[...] + _rms(m, g_ref[...])


def out_project(h, outs, w_out, g):
    rows, d = h.shape
    tm = min(rows, 1024)
    o_spec = pl.BlockSpec((tm, GROUP_WIDTH), lambda i: (i, 0))
    return pl.pallas_call(
        _outproj_kernel,
        grid=(rows // tm,),
        in_specs=[pl.BlockSpec((tm, d), lambda i: (i, 0)), o_spec, o_spec, o_spec, o_spec,
                  pl.BlockSpec((d, d), lambda i: (0, 0)), pl.BlockSpec((1, d), lambda i: (0, 0))],
        out_specs=pl.BlockSpec((tm, d), lambda i: (i, 0)),
        out_shape=jax.ShapeDtypeStruct((rows, d), F32),
        compiler_params=_cparams(1),
        name="out_project",
    )(h, *outs, w_out, g)


Q_MOBA, Q_FOX, Q_SB, Q_D1, Q_D2 = 0, 4, 8, 12, 16
N_MAPS = 20
S_FOX, S_D1, S_D2 = 0, 1, 2


def _bdot_qk(q, k):
    return lax.dot_general(q, k, (((2,), (2,)), ((0,), (0,))), preferred_element_type=F32)


def _bdot_pv(p, v):
    return lax.dot_general(p, v, (((2,), (1,)), ((0,), (0,))), preferred_element_type=F32)


def _head_const(values):
    return _select_by_index(lax.broadcasted_iota(jnp.int32, (GROUP_HEADS, 1, 1), 0), values)


def _sample_kernel(n_blocks, page, dec_seq,
                   pt_ref, q_ref, kn_ref, vn_ref, lfn_ref, lfnt_ref,
                   ck0_ref, ck1_ref, cv0_ref, cv1_ref, clt0_ref, clt1_ref, u_ref, lam_ref, sg_ref, o_ref,
                   qs_ref, m_ref, l_ref, acc_ref, accsb_ref, sbc_ref, fcar_ref, crow_ref,
                   bm_ref, bl_ref, bg_ref, bacc_ref):
    j = pl.program_id(1)
    blk_keys = 2 * page
    past_len = n_blocks * blk_keys
    blk = n_blocks - j
    qpos = past_len + lax.broadcasted_iota(jnp.int32, (1, dec_seq, 1), 1)
    kpos = blk * blk_keys + lax.broadcasted_iota(jnp.int32, (1, 1, blk_keys), 2)
    causal = kpos <= qpos
    strict = kpos < qpos
    dist = (qpos - kpos).astype(F32)

    @pl.when(j == 0)
    def _init():
        q = q_ref[...]
        lane = lax.broadcasted_iota(jnp.int32, (GROUP_HEADS, dec_seq, HEAD_DIM), 2)
        qs_ref[Q_MOBA:Q_D1] = q[0:3 * GROUP_HEADS] * (HEAD_DIM ** -0.5)
        qd = q[3 * GROUP_HEADS:4 * GROUP_HEADS]
        qs_ref[Q_D1:Q_D2] = jnp.where(lane < DIFF_QK_DIM, qd, 0.0)
        qs_ref[Q_D2:N_MAPS] = jnp.where(lane >= DIFF_QK_DIM, qd, 0.0)
        m_ref[...] = jnp.full_like(m_ref, NEG_INF)
        l_ref[...] = jnp.zeros_like(l_ref)
        acc_ref[...] = jnp.zeros_like(acc_ref)
        accsb_ref[...] = jnp.zeros_like(accsb_ref)
        sbc_ref[...] = jnp.zeros_like(sbc_ref)
        lfn = lfn_ref[...]
        tok = _row_iota(lfn.shape)
        cum = jnp.zeros_like(lfn)
        for i in range(dec_seq):
            cum = cum + jnp.where(tok >= i, lfn[i:i + 1, :], 0.0)
        crow_ref[...] = jnp.stack([cum[:, h:h + 1] for h in range(GROUP_HEADS)], axis=0)
        fcar_ref[...] = -jnp.sum(lfnt_ref[...], axis=-1, keepdims=True)

    def online(s, v, slot):
        m_prev = m_ref[slot]
        m_new = jnp.maximum(m_prev, jnp.max(s, axis=-1, keepdims=True))
        p = jnp.exp(s - m_new)
        alpha = jnp.exp(m_prev - m_new)
        l_ref[slot] = alpha * l_ref[slot] + jnp.sum(p, axis=-1, keepdims=True)
        acc_ref[slot] = alpha * acc_ref[slot] + _bdot_pv(p.astype(BF16), v)
        m_ref[slot] = m_new

    def block_step(kh, vh, lft, slot):
        q = qs_ref[...].astype(BF16)
        k_maps = jnp.concatenate([kh, kh[3 * GROUP_HEADS:4 * GROUP_HEADS]], axis=0)
        s_all = _bdot_qk(q, k_maps)

        s_raw = s_all[Q_MOBA:Q_FOX]
        s = jnp.where(causal, s_raw - _head_const(MOBA_SLOPES) * dist, NEG_INF)
        m = jnp.max(s, axis=-1, keepdims=True)
        p = jnp.exp(s - m)
        bm_ref[slot] = m
        bl_ref[slot] = jnp.sum(p, axis=-1, keepdims=True)
        bg_ref[slot] = jnp.sum(s_raw, axis=-1, keepdims=True)
        bacc_ref[slot] = _bdot_pv(p.astype(BF16), vh[0:GROUP_HEADS])

        suffix = _dot_f32_lhs(lft, u_ref[...]) + fcar_ref[...]
        suffix_h = jnp.stack([jnp.broadcast_to(suffix[h:h + 1, :], (dec_seq, blk_keys))
                              for h in range(GROUP_HEADS)], axis=0)
        s = s_all[Q_FOX:Q_SB] + crow_ref[...] + suffix_h
        online(jnp.where(causal, s, NEG_INF), vh[GROUP_HEADS:2 * GROUP_HEADS], S_FOX)
        fcar_ref[...] = fcar_ref[...] + jnp.sum(lft, axis=-1, keepdims=True)

        z = s_all[Q_SB:Q_D1]
        lk = jnp.where(strict, -_softplus(z), 0.0)
        lk2 = lk.reshape(GROUP_HEADS * dec_seq, blk_keys)
        lk_hi = lk2.astype(BF16)
        lk_lo = (lk2 - lk_hi.astype(F32)).astype(BF16)
        rest = (_dot(lk_hi, u_ref[...]) + _dot(lk_lo, u_ref[...])).reshape(GROUP_HEADS, dec_seq, blk_keys)
        a = jnp.where(strict, jnp.exp(z + lk + rest + sbc_ref[...]), 0.0)
        accsb_ref[...] = accsb_ref[...] + _bdot_pv(a.astype(BF16), vh[2 * GROUP_HEADS:3 * GROUP_HEADS])
        sbc_ref[...] = sbc_ref[...] + jnp.sum(lk, axis=-1, keepdims=True)

        bias_d = -_head_const(DIFF_SLOPES) * dist
        vd = vh[3 * GROUP_HEADS:4 * GROUP_HEADS]
        for first, state in ((Q_D1, S_D1), (Q_D2, S_D2)):
            s = s_all[first:first + GROUP_HEADS] * (DIFF_QK_DIM ** -0.5) + bias_d
            online(jnp.where(causal, s, NEG_INF), vd, state)

    @pl.when(j == 0)
    def _new_tokens():
        pad = jnp.zeros((N_HEADS, blk_keys - dec_seq, HEAD_DIM), F32)
        kh = jnp.concatenate([kn_ref[...], pad], axis=1).astype(BF16)
        vh = jnp.concatenate([vn_ref[...], pad], axis=1).astype(BF16)
        block_step(kh, vh, lfnt_ref[...], n_blocks)

    @pl.when(j > 0)
    def _past_block():
        k = jnp.concatenate([ck0_ref[...], ck1_ref[...]], axis=0)
        v = jnp.concatenate([cv0_ref[...], cv1_ref[...]], axis=0)
        kh = pltpu.einshape("shd->hsd", k).astype(BF16)
        vh = pltpu.einshape("shd->hsd", v).astype(BF16)
        lft = jnp.concatenate([clt0_ref[...], clt1_ref[...]], axis=1)
        block_step(kh, vh, lft, blk)

    @pl.when(j == n_blocks)
    def _finalize():
        gates = bg_ref[0:n_blocks]
        bidx = lax.broadcasted_iota(jnp.int32, gates.shape, 0)
        sel = jnp.zeros(gates.shape, F32)
        for _ in range(min(MOBA_TOPK, n_blocks)):
            mx = jnp.max(gates, axis=0, keepdims=True)
            first = jnp.min(jnp.where(gates == mx, bidx, n_blocks), axis=0, keepdims=True)
            hit = bidx == first
            sel = jnp.where(hit, 1.0, sel)
            gates = jnp.where(hit, -jnp.inf, gates)
        picked = sel > 0.0
        bm = bm_ref[0:n_blocks]
        m_own = bm_ref[n_blocks]
        m_fin = jnp.maximum(m_own, jnp.max(jnp.where(picked, bm, NEG_INF), axis=0))
        w = jnp.where(picked, jnp.exp(bm - m_fin), 0.0)
        w_own = jnp.exp(m_own - m_fin)
        den = w_own * bl_ref[n_blocks] + jnp.sum(w * bl_ref[0:n_blocks], axis=0)
        num = w_own * bacc_ref[n_blocks] + jnp.sum(w * bacc_ref[0:n_blocks], axis=0)
        o_ref[0:GROUP_HEADS] = num / den
        o_ref[GROUP_HEADS:2 * GROUP_HEADS] = acc_ref[S_FOX] / l_ref[S_FOX]
        o_ref[2 * GROUP_HEADS:3 * GROUP_HEADS] = accsb_ref[...]
        lp = lam_ref[...]
        lam_init = lp[4:5, 0:1]
        lam = (jnp.exp(jnp.sum(lp[0:1] * lp[1:2], axis=-1, keepdims=True))
               - jnp.exp(jnp.sum(lp[2:3] * lp[3:4], axis=-1, keepdims=True)) + lam_init)
        o_d = acc_ref[S_D1] / l_ref[S_D1] - lam * (acc_ref[S_D2] / l_ref[S_D2])
        ms = jnp.mean(o_d * o_d, axis=-1, keepdims=True)
        o_ref[3 * GROUP_HEADS:4 * GROUP_HEADS] = o_d * lax.rsqrt(ms + RMS_EPS) * sg_ref[...] * (1.0 - lam_init)


def sample_attention(layer, q, k_new, v_new, lf_new, cache_k, cache_v, cache_lft, page_table, lam_p, subln):
    b, n_heads, dec_seq, hd = q.shape
    n_pages = page_table.shape[1]
    page = cache_k.shape[2]
    blk_keys = 2 * page
    assert blk_keys == MOBA_BLOCK and n_pages % 2 == 0 and n_heads == N_HEADS and hd == HEAD_DIM
    n_blocks = n_pages // 2
    lf_new_t = jnp.pad(jnp.swapaxes(lf_new[:, :, 0:GROUP_HEADS], 1, 2),
                       ((0, 0), (0, 8 - GROUP_HEADS), (0, blk_keys - dec_seq)))
    idx = jnp.arange(blk_keys)
    u = (idx[:, None] > idx[None, :]).astype(BF16)
    subln3 = subln[:, 0:HEAD_DIM].reshape(1, 1, HEAD_DIM)

    def page_idx(which):
        return lambda bi, j, pt: (layer, pt[bi, 2 * (n_blocks - jnp.maximum(j, 1)) + which], 0, 0, 0)

    def lf_idx(which):
        return lambda bi, j, pt: (layer, pt[bi, 2 * (n_blocks - jnp.maximum(j, 1)) + which], 0, 0)

    per_batch4 = lambda bi, j, pt: (bi, 0, 0, 0)
    per_batch3 = lambda bi, j, pt: (bi, 0, 0)
    const2 = lambda bi, j, pt: (0, 0)
    head_block = pl.BlockSpec((None, n_heads, dec_seq, hd), per_batch4)
    page_shape = (None, None, page, n_heads, hd)
    stat = lambda n: pltpu.VMEM((n, GROUP_HEADS, dec_seq, 1), F32)
    grid_spec = pltpu.PrefetchScalarGridSpec(
        num_scalar_prefetch=1,
        grid=(b, n_blocks + 1),
        in_specs=[
            head_block, head_block, head_block,
            pl.BlockSpec((None, dec_seq, LANES), per_batch3),
            pl.BlockSpec((None, 8, blk_keys), per_batch3),
            pl.BlockSpec(page_shape, page_idx(0)), pl.BlockSpec(page_shape, page_idx(1)),
            pl.BlockSpec(page_shape, page_idx(0)), pl.BlockSpec(page_shape, page_idx(1)),
            pl.BlockSpec((None, None, 8, page), lf_idx(0)), pl.BlockSpec((None, None, 8, page), lf_idx(1)),
            pl.BlockSpec((blk_keys, blk_keys), const2),
            pl.BlockSpec((8, LANES), const2),
            pl.BlockSpec((1, 1, hd), lambda bi, j, pt: (0, 0, 0)),
        ],
        out_specs=head_block,
        scratch_shapes=[
            pltpu.VMEM((N_MAPS, dec_seq, hd), F32),
            stat(3), stat(3), pltpu.VMEM((3, GROUP_HEADS, dec_seq, hd), F32),
            pltpu.VMEM((GROUP_HEADS, dec_seq, hd), F32),
            pltpu.VMEM((GROUP_HEADS, dec_seq, 1), F32),
            pltpu.VMEM((8, 1), F32),
            pltpu.VMEM((GROUP_HEADS, dec_seq, 1), F32),
            stat(n_blocks + 1), stat(n_blocks + 1), stat(n_blocks + 1),
            pltpu.VMEM((n_blocks + 1, GROUP_HEADS, dec_seq, hd), F32),
        ],
    )
    return pl.pallas_call(
        functools.partial(_sample_kernel, n_blocks, page, dec_seq),
        grid_spec=grid_spec,
        out_shape=jax.ShapeDtypeStruct((b, n_heads, dec_seq, hd), F32),
        compiler_params=_cparams(2),
        name="sample_attn",
    )(page_table, q, k_new, v_new, lf_new, lf_new_t, cache_k, cache_k, cache_v, cache_v, cache_lft, cache_lft,
      u, lam_p, subln3)


def kernel(x_prompt, x_sample, cache_k, cache_v, cache_logf, page_table, norm_g, w_in, b_f, w_out,
           diff_lq1, diff_lk1, diff_lq2, diff_lk2, diff_subln_g, ffn_w_gate, ffn_w_up, ffn_w_down):
    depth = norm_g.shape[0]
    bp, seq, d = x_prompt.shape
    bs, dec_seq, _ = x_sample.shape
    assert bp == 1 and d == D_MODEL
    mix = N_HEADS * HEAD_DIM

    def head_major(a):
        return jnp.swapaxes(a.reshape(bs, dec_seq, N_HEADS, HEAD_DIM), 1, 2)

    clt = jnp.pad(jnp.swapaxes(cache_logf, 2, 3), ((0, 0), (0, 0), (0, 8 - GROUP_HEADS), (0, 0)))

    moba_qa, moba_ka = alibi_aug(seq, MOBA_SLOPES)
    diff_qa, diff_ka = alibi_aug(seq, DIFF_SLOPES)

    hp = x_prompt.reshape(seq, d)
    hs = x_sample.reshape(bs * dec_seq, d)
    pk, pv, pf, sk, sv, sf = [], [], [], [], [], []
    for l in range(depth):
        lam_init = 0.8 - 0.6 * math.exp(-0.3 * l)
        g = norm_g[l].reshape(6, 1, d)
        wg = ffn_w_gate[l].astype(BF16)
        wu = ffn_w_up[l].astype(BF16)
        wd = ffn_w_down[l].astype(BF16)
        w_qkv = w_in[l, :, 0:3 * mix].astype(BF16)
        w_f = jnp.pad(w_in[l, :, 3 * mix:], ((0, 0), (0, LANES - GROUP_HEADS))).astype(BF16)
        bias_f = jnp.pad(b_f[l].astype(F32), (0, LANES - GROUP_HEADS)).reshape(1, LANES)
        wo = w_out[l].astype(BF16)
        lam_p = _lam_params(diff_lq1[l], diff_lk1[l], diff_lq2[l], diff_lk2[l], lam_init)
        subln = jnp.tile(diff_subln_g[l].astype(F32).reshape(1, HEAD_DIM), (1, 2))

        hp = ffn_block(hp, g[0], g[1], wg[0], wu[0], wd[0])
        q, kf, vf, kb, vt, lf, kmean = project(hp, g[2], w_qkv, w_f, bias_f, True)
        fox_qa, fox_ka = fox_aug(*cumsum_logf(lf))
        n_blk = seq // MOBA_BLOCK
        km = jnp.pad(kmean.reshape(n_blk, GROUP_WIDTH), ((0, LANES - n_blk), (0, 0))).astype(BF16)
        outs = [prompt_attention(G_MOBA, q, kb, vt, (moba_qa, moba_ka, km)),
                prompt_attention(G_FOX, q, kb, vt, (fox_qa, fox_ka)),
                prompt_attention(G_SB, q, kb, vt, ()),
                prompt_attention(G_DIFF, q, kb, vt, (diff_qa, diff_ka, lam_p, subln))]
        hp = out_project(hp, outs, wo, g[3])
        hp = ffn_block(hp, g[4], g[5], wg[1], wu[1], wd[1])
        pk.append(kf.reshape(bp, seq, N_HEADS, HEAD_DIM))
        pv.append(vf.reshape(bp, seq, N_HEADS, HEAD_DIM))
        pf.append(lf[:, 0:GROUP_HEADS].reshape(bp, seq, GROUP_HEADS))

        hs = ffn_block(hs, g[0], g[1], wg[0], wu[0], wd[0])
        q, kf, vf, kb, vb, lf = project(hs, g[2], w_qkv, w_f, bias_f, False)
        o = sample_attention(l, head_major(q), head_major(kf), head_major(vf), lf.reshape(bs, dec_seq, LANES),
                             cache_k, cache_v, clt, page_table, lam_p, subln)
        o = jnp.swapaxes(o, 1, 2).reshape(bs * dec_seq, d).astype(BF16)
        outs = [o[:, i * GROUP_WIDTH:(i + 1) * GROUP_WIDTH] for i in range(4)]
        hs = out_project(hs, outs, wo, g[3])
        hs = ffn_block(hs, g[4], g[5], wg[1], wu[1], wd[1])
        sk.append(kf.reshape(bs, dec_seq, N_HEADS, HEAD_DIM))
        sv.append(vf.reshape(bs, dec_seq, N_HEADS, HEAD_DIM))
        sf.append(lf[:, 0:GROUP_HEADS].reshape(bs, dec_seq, GROUP_HEADS))

    return (hp.reshape(bp, seq, d), hs.reshape(bs, dec_seq, d),
            jnp.stack(pk), jnp.stack(pv), jnp.stack(pf), jnp.stack(sk), jnp.stack(sv), jnp.stack(sf))
```

```python
import functools
import math

import numpy as np
import jax
import jax.numpy as jnp
from jax import lax
from jax.experimental import pallas as pl
from jax.experimental.pallas import tpu as pltpu

F32 = jnp.float32
BF16 = jnp.bfloat16

D_MODEL = 1024
N_HEADS = 16
HEAD_DIM = 64
GROUP_HEADS = 4
GROUP_WIDTH = GROUP_HEADS * HEAD_DIM
PAIR_WIDTH = 2 * HEAD_DIM
DIFF_QK_DIM = HEAD_DIM // 2
MOBA_BLOCK = 256
MOBA_TOPK = 3
RMS_EPS = 1e-6
NEG_INF = -1e30
LANES = 128
G_MOBA, G_FOX, G_SB, G_DIFF = 0, 1, 2, 3
MOBA_SLOPES = tuple(2.0 ** -(2 * h + 1) for h in range(GROUP_HEADS))
DIFF_SLOPES = tuple(2.0 ** -(2 * h + 2) for h in range(GROUP_HEADS))
VMEM_LIMIT = 56 * 1024 * 1024
LOG2E = 1.4426950408889634
ONES_ROWS = 16
SB_DEAD = -104.0


def _cparams(n_axes):
    return pltpu.CompilerParams(dimension_semantics=("arbitrary",) * n_axes,
                                vmem_limit_bytes=VMEM_LIMIT)


def _rms(x, g):
    return x * lax.rsqrt(jnp.mean(x * x, axis=-1, keepdims=True) + RMS_EPS) * g


def _dot(a, b):
    return jnp.dot(a, b, preferred_element_type=F32)


def _dot_nt(a, b):
    return lax.dot_general(a, b, (((1,), (1,)), ((), ())), preferred_element_type=F32)


def _split3(a):
    a1 = a.astype(BF16)
    r1 = a - a1.astype(F32)
    a2 = r1.astype(BF16)
    a3 = (r1 - a2.astype(F32)).astype(BF16)
    return a1, a2, a3


def _dot_f32_lhs(a, b01):
    a1, a2, a3 = _split3(a)
    return _dot(a1, b01) + _dot(a2, b01) + _dot(a3, b01)


def _dot_f32_rhs(a01, b):
    b1, b2, b3 = _split3(b)
    return _dot(a01, b1) + _dot(a01, b2) + _dot(a01, b3)


def _log_sigmoid(x):
    return jnp.minimum(x, 0.0) - jnp.log1p(jnp.exp(-jnp.abs(x)))


def _softplus(x):
    return jnp.maximum(x, 0.0) + jnp.log1p(jnp.exp(-jnp.abs(x)))


def _lane_iota(shape):
    return lax.broadcasted_iota(jnp.int32, shape, len(shape) - 1)


def _row_iota(shape):
    return lax.broadcasted_iota(jnp.int32, shape, len(shape) - 2)


def _select_by_index(idx, values):
    out = jnp.full(jnp.shape(idx), values[-1], F32)
    for i in range(len(values) - 2, -1, -1):
        out = jnp.where(idx == i, values[i], out)
    return out


def _ffn_kernel(x_ref, gpre_ref, gpost_ref, wg_ref, wu_ref, wd_ref, o_ref, xn_ref, acc_ref):
    j = pl.program_id(1)

    @pl.when(j == 0)
    def _():
        xn_ref[...] = _rms(x_ref[...], gpre_ref[...]).astype(BF16)
        acc_ref[...] = jnp.zeros_like(acc_ref)

    xn = xn_ref[...]
    gate = _dot(xn, wg_ref[...])
    up = _dot(xn, wu_ref[...])
    mid = gate * jax.nn.sigmoid(gate) * up
    acc_ref[...] += _dot(mid.astype(BF16), wd_ref[...])

    @pl.when(j == pl.num_programs(1) - 1)
    def _():
        o_ref[...] = x_ref[...] + 0.5 * _rms(acc_ref[...], gpost_ref[...])


def _ffn_tiles(rows, d_ff):
    tm = min(rows, 512)
    tf = d_ff
    for cand in (1408, 1024, 512, 256, 128):
        if d_ff % cand == 0:
            tf = cand
            break
    return tm, tf


def ffn_block(x, g_pre, g_post, w_gate, w_up, w_down):
    rows, d = x.shape
    d_ff = w_gate.shape[1]
    tm, tf = _ffn_tiles(rows, d_ff)
    return pl.pallas_call(
        _ffn_kernel,
        grid=(rows // tm, d_ff // tf),
        in_specs=[
            pl.BlockSpec((tm, d), lambda i, j: (i, 0)),
            pl.BlockSpec((1, d), lambda i, j: (0, 0)),
            pl.BlockSpec((1, d), lambda i, j: (0, 0)),
            pl.BlockSpec((d, tf), lambda i, j: (0, j)),
            pl.BlockSpec((d, tf), lambda i, j: (0, j)),
            pl.BlockSpec((tf, d), lambda i, j: (j, 0)),
        ],
        out_specs=pl.BlockSpec((tm, d), lambda i, j: (i, 0)),
        out_shape=jax.ShapeDtypeStruct((rows, d), F32),
        scratch_shapes=[pltpu.VMEM((tm, d), BF16), pltpu.VMEM((tm, d), F32)],
        compiler_params=_cparams(2),
        name="ffn_block",
    )(x, g_pre, g_post, w_gate, w_up, w_down)


def _proj_kernel(with_kmean, h_ref, g_ref, w_ref, wf_ref, bf_ref, *out_refs):
    if with_kmean:
        q_ref, kf_ref, vf_ref, kb_ref, vb_ref, lf_ref, km_ref = out_refs
    else:
        q_ref, kf_ref, vf_ref, kb_ref, vb_ref, lf_ref = out_refs
    hn = _rms(h_ref[...], g_ref[...]).astype(BF16)
    q = _dot(hn, w_ref[:, 0:D_MODEL])
    q_ref[...] = q.astype(q_ref.dtype)
    k = _dot(hn, w_ref[:, D_MODEL:2 * D_MODEL])
    kf_ref[...] = k
    kb_ref[...] = k.astype(BF16)
    v = _dot(hn, w_ref[:, 2 * D_MODEL:3 * D_MODEL])
    vf_ref[...] = v
    vb_ref[...] = (v.T if with_kmean else v).astype(BF16)
    lf_ref[...] = _log_sigmoid(_dot(hn, wf_ref[...]) + bf_ref[...])
    if with_kmean:
        for i in range(k.shape[0] // MOBA_BLOCK):
            blk = k[i * MOBA_BLOCK:(i + 1) * MOBA_BLOCK, 0:GROUP_WIDTH]
            km_ref[i] = jnp.mean(blk, axis=0, keepdims=True)


def project(h, g, w_qkv, w_f, b_f, with_kmean):
    rows, d = h.shape
    tm = min(rows, 512)
    row_spec = pl.BlockSpec((tm, d), lambda i: (i, 0))
    q_dtype = BF16 if with_kmean else F32
    out_shapes = [jax.ShapeDtypeStruct((rows, d), q_dtype),
                  jax.ShapeDtypeStruct((rows, d), F32), jax.ShapeDtypeStruct((rows, d), F32),
                  jax.ShapeDtypeStruct((rows, d), BF16),
                  jax.ShapeDtypeStruct((d, rows) if with_kmean else (rows, d), BF16),
                  jax.ShapeDtypeStruct((rows, LANES), F32)]
    v_spec = pl.BlockSpec((d, tm), lambda i: (0, i)) if with_kmean else row_spec
    out_specs = [row_spec] * 4 + [v_spec, pl.BlockSpec((tm, LANES), lambda i: (i, 0))]
    if with_kmean:
        out_shapes.append(jax.ShapeDtypeStruct((rows // MOBA_BLOCK, 1, GROUP_WIDTH), F32))
        out_specs.append(pl.BlockSpec((tm // MOBA_BLOCK, 1, GROUP_WIDTH), lambda i: (i, 0, 0)))
    return pl.pallas_call(
        functools.partial(_proj_kernel, with_kmean),
        grid=(rows // tm,),
        in_specs=[
            row_spec,
            pl.BlockSpec((1, d), lambda i: (0, 0)),
            pl.BlockSpec((d, 3 * d), lambda i: (0, 0)),
            pl.BlockSpec((d, LANES), lambda i: (0, 0)),
            pl.BlockSpec((1, LANES), lambda i: (0, 0)),
        ],
        out_specs=out_specs,
        out_shape=out_shapes,
        compiler_params=_cparams(1),
        name="project",
    )(h, g, w_qkv, w_f, b_f)


def _cumsum_kernel(lf_ref, tri_ref, p1_ref, p2_ref, p3_ref):
    n_chunks = lf_ref.shape[0] // LANES
    tri = tri_ref[...]

    def body(i, carry):
        rows = pl.ds(pl.multiple_of(i * LANES, LANES), LANES)
        c = _dot_f32_rhs(tri, lf_ref[rows, :]) + carry
        p1, p2, p3 = _split3(c * LOG2E)
        p1_ref[rows, :] = p1
        p2_ref[rows, :] = p2
        p3_ref[rows, :] = p3
        return c[LANES - 1:LANES, :]

    lax.fori_loop(0, n_chunks, body, jnp.zeros((1, LANES), F32))


def cumsum_logf(lf):
    rows = lf.shape[0]
    idx = jnp.arange(LANES)
    tri = (idx[None, :] <= idx[:, None]).astype(BF16)
    piece = jax.ShapeDtypeStruct((rows, LANES), BF16)
    return pl.pallas_call(
        _cumsum_kernel,
        out_shape=[piece, piece, piece],
        compiler_params=pltpu.CompilerParams(vmem_limit_bytes=VMEM_LIMIT),
        name="cumsum_logf",
    )(lf, tri)


def _split3_const(x):
    x = np.float32(x)
    a1 = np.float32(np.asarray(x, dtype=BF16))
    a2 = np.float32(np.asarray(x - a1, dtype=BF16))
    a3 = np.float32(np.asarray(x - a1 - a2, dtype=BF16))
    return float(a1), float(a2), float(a3)


def _aug_base(h):
    return HEAD_DIM if h % 2 == 0 else 0


def _place_lanes(columns, base):
    cols = jnp.stack(columns, axis=1)
    return jnp.pad(cols, ((0, 0), (base, LANES - base - len(columns))))


def alibi_aug(t, slopes):
    pos = jnp.arange(t, dtype=jnp.int32)
    hi = ((pos // LANES) * LANES).astype(F32)
    lo = (pos % LANES).astype(F32)
    qa_all, ka_all = [], []
    for h, slope in enumerate(slopes):
        c = [jnp.full((t,), ci, F32) for ci in _split3_const(LOG2E * slope)]
        qa_all.append(_place_lanes(c + c + [-hi] * 3 + [-lo] * 3, _aug_base(h)))
        ka_all.append(_place_lanes([hi] * 3 + [lo] * 3 + c + c, _aug_base(h)))
    return jnp.stack(qa_all).astype(BF16), jnp.stack(ka_all).astype(BF16)


def fox_aug(p1, p2, p3):
    t = p1.shape[0]
    one = jnp.ones((t,), BF16)
    qa_all, ka_all = [], []
    for h in range(GROUP_HEADS):
        pieces = [p[:, h] for p in (p1, p2, p3)]
        qa_all.append(_place_lanes(pieces + [-one] * 3, _aug_base(h)))
        ka_all.append(_place_lanes([one] * 3 + pieces, _aug_base(h)))
    return jnp.stack(qa_all), jnp.stack(ka_all)


def _attn_kernel(mode, tq, tk, qi_tab, kj_tab, *refs):
    if mode == G_MOBA:
        q_ref, k_ref, vt_ref, qa_ref, ka_ref, km_ref, o_ref, qaug_ref, m_ref, acc_ref, sel_ref = refs
    elif mode == G_FOX:
        q_ref, k_ref, vt_ref, qa_ref, ka_ref, o_ref, qaug_ref, m_ref, acc_ref = refs
    elif mode == G_SB:
        q_ref, k_ref, vt_ref, u_ref, o_ref, qaug_ref, carry_ref, acc_ref = refs
    else:
        q_ref, k_ref, vt_ref, qa_ref, ka_ref, lam_ref, sg_ref, o_ref, qaug_ref, m_ref, acc_ref = refs

    step_id = pl.program_id(1)
    qi = qi_tab[step_id]
    kj = kj_tab[step_id]
    last_k = (qi * tq + tq - 1) // tk
    kt = last_k - kj
    lane_q = _lane_iota((tq, PAIR_WIDTH))
    lane_row = _lane_iota((1, PAIR_WIDTH))
    qpos = qi * tq + _lane_iota((1, tq))

    @pl.when(kj == 0)
    def _init():
        acc_ref[...] = jnp.zeros_like(acc_ref)
        if mode == G_SB:
            carry_ref[...] = jnp.zeros_like(carry_ref)
        else:
            m_ref[...] = jnp.full_like(m_ref, NEG_INF)
        q = q_ref[...].astype(F32)
        for hl in range(2):
            own = (lane_q // HEAD_DIM) == hl
            if mode == G_SB:
                qaug_ref[hl] = jnp.where(own, q * (HEAD_DIM ** -0.5), 0.0).astype(BF16)
            elif mode == G_DIFF:
                qa = qa_ref[hl].astype(F32)
                qs = q * (DIFF_QK_DIM ** -0.5 * LOG2E)
                for mp in range(2):
                    lo = hl * HEAD_DIM + mp * DIFF_QK_DIM
                    keep = (lane_q >= lo) & (lane_q < lo + DIFF_QK_DIM)
                    qaug_ref[hl * 2 + mp] = jnp.where(keep, qs, jnp.where(own, 0.0, qa)).astype(BF16)
            else:
                qaug_ref[hl] = jnp.where(own, q * (HEAD_DIM ** -0.5 * LOG2E), qa_ref[hl].astype(F32)).astype(BF16)
        if mode == G_MOBA:
            blk_iota = _row_iota((LANES, tq))
            qblk = qpos // MOBA_BLOCK
            for hl in range(2):
                own = (lane_q // HEAD_DIM) == hl
                qg = jnp.where(own, q, 0.0).astype(BF16)
                g = _dot_nt(km_ref[...], qg)
                g = jnp.where(blk_iota < qblk, g, NEG_INF)
                sel = jnp.zeros((LANES, tq), F32)
                for _ in range(MOBA_TOPK):
                    mx = jnp.max(g, axis=0, keepdims=True)
                    first = jnp.min(jnp.where(g == mx, blk_iota, LANES), axis=0, keepdims=True)
                    hit = blk_iota == first
                    sel = jnp.where(hit, 1.0, sel)
                    g = jnp.where(hit, -jnp.inf, g)
                sel_ref[hl] = jnp.where(blk_iota < qblk, sel, 0.0)

    def online(s, vt_h, idx):
        m_prev = m_ref[idx]
        m_new = jnp.maximum(m_prev, jnp.max(s, axis=0, keepdims=True))
        p = jnp.exp2(s - m_new)
        alpha = jnp.exp2(m_prev - m_new)
        acc_ref[idx] = alpha * acc_ref[idx] + _dot(vt_h, p.astype(BF16))
        m_ref[idx] = m_new

    def step(diag):
        k = k_ref[...]
        kpos = kt * tk + _row_iota((tk, 1))
        items = []
        for hl in range(2):
            own_row = ((lane_row // HEAD_DIM) == hl).astype(F32).astype(BF16)
            vt_h = vt_ref[hl * HEAD_DIM:(hl + 1) * HEAD_DIM, :]
            if mode == G_SB:
                z = _dot_nt(k * own_row, qaug_ref[hl])
                lk = -_softplus(z)
                if diag:
                    strict = kpos < qpos
                    lk = jnp.where(strict, lk, 0.0)
                carry = carry_ref[hl]
                for hb in range(tk // MOBA_BLOCK - 1, -1, -1):
                    rows = slice(hb * MOBA_BLOCK, (hb + 1) * MOBA_BLOCK)
                    lk_h = lk[rows]
                    lk_hi = lk_h.astype(BF16)
                    lk_lo = (lk_h - lk_hi.astype(F32)).astype(BF16)
                    rest = _dot(u_ref[...], lk_hi) + _dot(u_ref[...], lk_lo) + carry
                    a = jnp.exp(z[rows] + lk_h + rest)
                    if diag:
                        a = jnp.where(strict[rows], a, 0.0)
                    acc_ref[hl] = acc_ref[hl] + _dot(vt_h[:, rows], a.astype(BF16))
                    carry = carry + jnp.sum(lk_h, axis=0, keepdims=True)
                carry_ref[hl] = carry
                continue
            vt_h = jnp.concatenate([vt_h, jnp.ones((ONES_ROWS, tk), BF16)], axis=0)
            kaug = k * own_row + ka_ref[hl]
            n_maps = 2 if mode == G_DIFF else 1
            for mp in range(n_maps):
                items.append((hl, hl * n_maps + mp, kaug, vt_h))

        def scores(item):
            hl, idx, kaug, _ = item
            s = _dot_nt(kaug, qaug_ref[idx])
            if mode == G_MOBA:
                parts = []
                for b in range(tk // MOBA_BLOCK):
                    blk = kt * (tk // MOBA_BLOCK) + b
                    rows = slice(b * MOBA_BLOCK, (b + 1) * MOBA_BLOCK)
                    allowed = sel_ref[hl, pl.ds(blk, 1), :] > 0.0
                    if diag:
                        allowed = allowed | (((qpos // MOBA_BLOCK) == blk) & (kpos[rows] <= qpos))
                    parts.append(jnp.where(allowed, s[rows], NEG_INF))
                s = parts[0] if len(parts) == 1 else jnp.concatenate(parts, axis=0)
            elif diag:
                s = jnp.where(kpos <= qpos, s, NEG_INF)
            return s

        if items:
            s_next = scores(items[0])
            for i, item in enumerate(items):
                s_cur = s_next
                if i + 1 < len(items):
                    s_next = scores(items[i + 1])
                online(s_cur, item[3], item[1])

    touches_diag = (kt * tk + tk - 1) > (qi * tq)

    def run():
        @pl.when(touches_diag)
        def _diag():
            step(True)

        @pl.when(jnp.logical_not(touches_diag))
        def _interior():
            step(False)

    if mode == G_SB:
        pl.when(jnp.max(carry_ref[...]) > SB_DEAD)(run)
    else:
        run()

    @pl.when(kj == last_k)
    def _finalize():
        if mode == G_SB:
            o = jnp.concatenate([acc_ref[0], acc_ref[1]], axis=0).T
        elif mode == G_DIFF:
            lp = lam_ref[...]
            lam_init = lp[4:5, 0:1]
            lam = (jnp.exp(jnp.sum(lp[0:1] * lp[1:2], axis=-1, keepdims=True))
                   - jnp.exp(jnp.sum(lp[2:3] * lp[3:4], axis=-1, keepdims=True)) + lam_init)
            outs = []
            for hl in range(2):
                a1, a2 = acc_ref[2 * hl], acc_ref[2 * hl + 1]
                o_h = (a1[0:HEAD_DIM] / a1[HEAD_DIM:HEAD_DIM + 1]
                       - lam * (a2[0:HEAD_DIM] / a2[HEAD_DIM:HEAD_DIM + 1]))
                ms = jnp.mean(o_h * o_h, axis=0, keepdims=True)
                outs.append(o_h * lax.rsqrt(ms + RMS_EPS))
            o = jnp.concatenate(outs, axis=0).T * sg_ref[...] * (1.0 - lam_init)
        else:
            a0, a1 = acc_ref[0], acc_ref[1]
            o = jnp.concatenate([a0[0:HEAD_DIM] / a0[HEAD_DIM:HEAD_DIM + 1],
                                 a1[0:HEAD_DIM] / a1[HEAD_DIM:HEAD_DIM + 1]], axis=0).T
        o_ref[...] = o.astype(o_ref.dtype)


def _lam_params(lq1, lk1, lq2, lk2, lam_init):
    rows = [jnp.pad(a.astype(F32), (0, LANES - a.shape[0])) for a in (lq1, lk1, lq2, lk2)]
    rows.append(jnp.full((LANES,), lam_init, F32))
    rows += [jnp.zeros((LANES,), F32)] * 3
    return jnp.stack(rows)


def _attn_tiles(mode, t):
    tq = min(512, t)
    tk = min(512 if mode == G_SB else 2048, t)
    return tq, tk


def prompt_attention(mode, q, k, vt, extras):
    t = q.shape[0]
    tq, tk = _attn_tiles(mode, t)
    nq = t // tq
    col = mode * 2

    def last_tile(qi):
        return (qi * tq + tq - 1) // tk

    qi_list, kj_list = [], []
    for qi in range(nq):
        for kj in range(last_tile(qi) + 1):
            qi_list.append(qi)
            kj_list.append(kj)
    qi_tab = jnp.asarray(np.array(qi_list, np.int32))
    kj_tab = jnp.asarray(np.array(kj_list, np.int32))

    def key_tile(s, qt, kt):
        return last_tile(qt[s]) - kt[s]

    const = lambda p, s, qt, kt: (0, 0)
    in_specs = [pl.BlockSpec((tq, PAIR_WIDTH), lambda p, s, qt, kt: (qt[s], col + p)),
                pl.BlockSpec((tk, PAIR_WIDTH), lambda p, s, qt, kt: (key_tile(s, qt, kt), col + p)),
                pl.BlockSpec((PAIR_WIDTH, tk), lambda p, s, qt, kt: (col + p, key_tile(s, qt, kt)))]
    aug_specs = [pl.BlockSpec((2, tq, LANES), lambda p, s, qt, kt: (p, qt[s], 0)),
                 pl.BlockSpec((2, tk, LANES), lambda p, s, qt, kt: (p, key_tile(s, qt, kt), 0))]
    n_state = 4 if mode == G_DIFF else 2
    qaug = pltpu.VMEM((n_state, tq, PAIR_WIDTH), BF16)
    stat = pltpu.VMEM((n_state, 1, tq), F32)
    acc = pltpu.VMEM((n_state, HEAD_DIM if mode == G_SB else HEAD_DIM + ONES_ROWS, tq), F32)
    if mode == G_MOBA:
        in_specs += aug_specs + [pl.BlockSpec((LANES, PAIR_WIDTH), lambda p, s, qt, kt: (0, p))]
        scratch = [qaug, stat, acc, pltpu.VMEM((2, LANES, tq), F32)]
    elif mode == G_FOX:
        in_specs += aug_specs
        scratch = [qaug, stat, acc]
    elif mode == G_SB:
        idx = jnp.arange(MOBA_BLOCK)
        extras = ((idx[None, :] > idx[:, None]).astype(BF16),)
        in_specs.append(pl.BlockSpec((MOBA_BLOCK, MOBA_BLOCK), const))
        scratch = [qaug, stat, acc]
    else:
        in_specs += aug_specs + [pl.BlockSpec((8, LANES), const), pl.BlockSpec((1, PAIR_WIDTH), const)]
        scratch = [qaug, stat, acc]
    grid_spec = pltpu.PrefetchScalarGridSpec(
        num_scalar_prefetch=2,
        grid=(2, len(qi_list)),
        in_specs=in_specs,
        out_specs=pl.BlockSpec((tq, PAIR_WIDTH), lambda p, s, qt, kt: (qt[s], p)),
        scratch_shapes=scratch,
    )
    return pl.pallas_call(
        functools.partial(_attn_kernel, mode, tq, tk),
        grid_spec=grid_spec,
        out_shape=jax.ShapeDtypeStruct((t, GROUP_WIDTH), BF16),
        compiler_params=_cparams(2),
        name="prompt_attn_%d" % mode,
    )(qi_tab, kj_tab, q, k, vt, *extras)


def _outproj_kernel(h_ref, o0_ref, o1_ref, o2_ref, o3_ref, w_ref, g_ref, out_ref):
    m = _dot(o0_ref[...], w_ref[0:GROUP_WIDTH, :])
    for gi, o_ref in enumerate((o1_ref, o2_ref, o3_ref), start=1):
        m = m + _dot(o_ref[...], w_ref[gi * GROUP_WIDTH:(gi + 1) * GROUP_WIDTH, :])
    out_ref[...] = h_ref[...] + _rms(m, g_ref[...])


def out_project(h, outs, w_out, g):
    rows, d = h.shape
    tm = min(rows, 1024)
    o_spec = pl.BlockSpec((tm, GROUP_WIDTH), lambda i: (i, 0))
    return pl.pallas_call(
        _outproj_kernel,
        grid=(rows // tm,),
        in_specs=[pl.BlockSpec((tm, d), lambda i: (i, 0)), o_spec, o_spec, o_spec, o_spec,
                  pl.BlockSpec((d, d), lambda i: (0, 0)), pl.BlockSpec((1, d), lambda i: (0, 0))],
        out_specs=pl.BlockSpec((tm, d), lambda i: (i, 0)),
        out_shape=jax.ShapeDtypeStruct((rows, d), F32),
        compiler_params=_cparams(1),
        name="out_project",
    )(h, *outs, w_out, g)


Q_MOBA, Q_FOX, Q_SB, Q_D1, Q_D2 = 0, 4, 8, 12, 16
N_MAPS = 20
S_FOX, S_D1, S_D2 = 0, 1, 2


def _bdot_nt(a, b):
    return lax.dot_general(a, b, (((2,), (2,)), ((0,), (0,))), preferred_element_type=F32)


def _bdot_nn(a, b):
    return lax.dot_general(a, b, (((2,), (1,)), ((0,), (0,))), preferred_element_type=F32)


def _head_const(values):
    return _select_by_index(lax.broadcasted_iota(jnp.int32, (GROUP_HEADS, 1, 1), 0), values)


def _sample_kernel(n_blocks, page, dec_seq,
                   pt_ref, q_ref, kn_ref, vn_ref, lfn_ref, lfnt_ref,
                   ck0_ref, ck1_ref, cv0_ref, cv1_ref, clt0_ref, clt1_ref, u_ref, lam_ref, sg_ref, o_ref,
                   qs_ref, m_ref, l_ref, acc_ref, accsb_ref, sbc_ref, fcar_ref, crow_ref,
                   bm_ref, bl_ref, bg_ref, bacc_ref):
    j = pl.program_id(1)
    blk_keys = 2 * page
    past_len = n_blocks * blk_keys
    blk = n_blocks - j
    qpos = past_len + lax.broadcasted_iota(jnp.int32, (1, dec_seq, 1), 1)
    kpos = blk * blk_keys + lax.broadcasted_iota(jnp.int32, (1, 1, blk_keys), 2)
    causal = kpos <= qpos
    strict = kpos < qpos
    dist = (qpos - kpos).astype(F32)

    @pl.when(j == 0)
    def _init():
        q = q_ref[...]
        lane = lax.broadcasted_iota(jnp.int32, (GROUP_HEADS, dec_seq, HEAD_DIM), 2)
        qs_ref[Q_MOBA:Q_D1] = q[0:3 * GROUP_HEADS] * (HEAD_DIM ** -0.5)
        qd = q[3 * GROUP_HEADS:4 * GROUP_HEADS]
        qs_ref[Q_D1:Q_D2] = jnp.where(lane < DIFF_QK_DIM, qd, 0.0)
        qs_ref[Q_D2:N_MAPS] = jnp.where(lane >= DIFF_QK_DIM, qd, 0.0)
        m_ref[...] = jnp.full_like(m_ref, NEG_INF)
        l_ref[...] = jnp.zeros_like(l_ref)
        acc_ref[...] = jnp.zeros_like(acc_ref)
        accsb_ref[...] = jnp.zeros_like(accsb_ref)
        sbc_ref[...] = jnp.zeros_like(sbc_ref)
        lfn = lfn_ref[...]
        tok = _row_iota(lfn.shape)
        cum = jnp.zeros_like(lfn)
        for i in range(dec_seq):
            cum = cum + jnp.where(tok >= i, lfn[i:i + 1, :], 0.0)
        crow_ref[...] = jnp.stack([cum[:, h:h + 1] for h in range(GROUP_HEADS)], axis=0)
        fcar_ref[...] = -jnp.sum(lfnt_ref[...], axis=-1, keepdims=True)

    def online(s, v, slot):
        m_prev = m_ref[slot]
        m_new = jnp.maximum(m_prev, jnp.max(s, axis=-1, keepdims=True))
        p = jnp.exp(s - m_new)
        alpha = jnp.exp(m_prev - m_new)
        l_ref[slot] = alpha * l_ref[slot] + jnp.sum(p, axis=-1, keepdims=True)
        acc_ref[slot] = alpha * acc_ref[slot] + _bdot_nt(p.astype(BF16), v)
        m_ref[slot] = m_new

    def block_step(kh, vh, lft, slot):
        q = qs_ref[...].astype(BF16)
        k_maps = jnp.concatenate([kh, kh[3 * GROUP_HEADS:4 * GROUP_HEADS]], axis=0)
        s_all = _bdot_nn(q, k_maps)

        s_raw = s_all[Q_MOBA:Q_FOX]
        s = jnp.where(causal, s_raw - _head_const(MOBA_SLOPES) * dist, NEG_INF)
        m = jnp.max(s, axis=-1, keepdims=True)
        p = jnp.exp(s - m)
        bm_ref[slot] = m
        bl_ref[slot] = jnp.sum(p, axis=-1, keepdims=True)
        bg_ref[slot] = jnp.sum(s_raw, axis=-1, keepdims=True)
        bacc_ref[slot] = _bdot_nt(p.astype(BF16), vh[0:GROUP_HEADS])

        suffix = _dot_f32_lhs(lft, u_ref[...]) + fcar_ref[...]
        suffix_h = jnp.stack([jnp.broadcast_to(suffix[h:h + 1, :], (dec_seq, blk_keys))
                              for h in range(GROUP_HEADS)], axis=0)
        s = s_all[Q_FOX:Q_SB] + crow_ref[...] + suffix_h
        online(jnp.where(causal, s, NEG_INF), vh[GROUP_HEADS:2 * GROUP_HEADS], S_FOX)
        fcar_ref[...] = fcar_ref[...] + jnp.sum(lft, axis=-1, keepdims=True)

        z = s_all[Q_SB:Q_D1]
        lk = jnp.where(strict, -_softplus(z), 0.0)
        lk2 = lk.reshape(GROUP_HEADS * dec_seq, blk_keys)
        lk_hi = lk2.astype(BF16)
        lk_lo = (lk2 - lk_hi.astype(F32)).astype(BF16)
        rest = (_dot(lk_hi, u_ref[...]) + _dot(lk_lo, u_ref[...])).reshape(GROUP_HEADS, dec_seq, blk_keys)
        a = jnp.where(strict, jnp.exp(z + lk + rest + sbc_ref[...]), 0.0)
        accsb_ref[...] = accsb_ref[...] + _bdot_nt(a.astype(BF16), vh[2 * GROUP_HEADS:3 * GROUP_HEADS])
        sbc_ref[...] = sbc_ref[...] + jnp.sum(lk, axis=-1, keepdims=True)

        bias_d = -_head_const(DIFF_SLOPES) * dist
        vd = vh[3 * GROUP_HEADS:4 * GROUP_HEADS]
        for first, state in ((Q_D1, S_D1), (Q_D2, S_D2)):
            s = s_all[first:first + GROUP_HEADS] * (DIFF_QK_DIM ** -0.5) + bias_d
            online(jnp.where(causal, s, NEG_INF), vd, state)

    @pl.when(j == 0)
    def _new_tokens():
        pad = jnp.zeros((N_HEADS, HEAD_DIM, page), F32)
        kh = jnp.concatenate([kn_ref[...], pad], axis=2).astype(BF16)
        vh = jnp.concatenate([vn_ref[...], pad], axis=2).astype(BF16)
        block_step(kh, vh, lfnt_ref[...], n_blocks)

    @pl.when(j > 0)
    def _past_block():
        kh = jnp.concatenate([ck0_ref[...], ck1_ref[...]], axis=2).astype(BF16)
        vh = jnp.concatenate([cv0_ref[...], cv1_ref[...]], axis=2).astype(BF16)
        lft = jnp.concatenate([clt0_ref[...], clt1_ref[...]], axis=1)
        block_step(kh, vh, lft, blk)

    @pl.when(j == n_blocks)
    def _finalize():
        gates = bg_ref[0:n_blocks]
        bidx = lax.broadcasted_iota(jnp.int32, gates.shape, 0)
        sel = jnp.zeros(gates.shape, F32)
        for _ in range(min(MOBA_TOPK, n_blocks)):
            mx = jnp.max(gates, axis=0, keepdims=True)
            first = jnp.min(jnp.where(gates == mx, bidx, n_blocks), axis=0, keepdims=True)
            hit = bidx == first
            sel = jnp.where(hit, 1.0, sel)
            gates = jnp.where(hit, -jnp.inf, gates)
        picked = sel > 0.0
        bm = bm_ref[0:n_blocks]
        m_own = bm_ref[n_blocks]
        m_fin = jnp.maximum(m_own, jnp.max(jnp.where(picked, bm, NEG_INF), axis=0))
        w = jnp.where(picked, jnp.exp(bm - m_fin), 0.0)
        w_own = jnp.exp(m_own - m_fin)
        den = w_own * bl_ref[n_blocks] + jnp.sum(w * bl_ref[0:n_blocks], axis=0)
        num = w_own * bacc_ref[n_blocks] + jnp.sum(w * bacc_ref[0:n_blocks], axis=0)
        o_ref[0:GROUP_HEADS] = num / den
        o_ref[GROUP_HEADS:2 * GROUP_HEADS] = acc_ref[S_FOX] / l_ref[S_FOX]
        o_ref[2 * GROUP_HEADS:3 * GROUP_HEADS] = accsb_ref[...]
        lp = lam_ref[...]
        lam_init = lp[4:5, 0:1]
        lam = (jnp.exp(jnp.sum(lp[0:1] * lp[1:2], axis=-1, keepdims=True))
               - jnp.exp(jnp.sum(lp[2:3] * lp[3:4], axis=-1, keepdims=True)) + lam_init)
        o_d = acc_ref[S_D1] / l_ref[S_D1] - lam * (acc_ref[S_D2] / l_ref[S_D2])
        ms = jnp.mean(o_d * o_d, axis=-1, keepdims=True)
        o_ref[3 * GROUP_HEADS:4 * GROUP_HEADS] = o_d * lax.rsqrt(ms + RMS_EPS) * sg_ref[...] * (1.0 - lam_init)


def sample_attention(layer, q, k_new, v_new, lf_new, cache_k, cache_v, cache_lft, page_table, lam_p, subln):
    b, n_heads, dec_seq, hd = q.shape
    n_pages = page_table.shape[1]
    page = cache_k.shape[4]
    blk_keys = 2 * page
    assert blk_keys == MOBA_BLOCK and n_pages % 2 == 0 and n_heads == N_HEADS and hd == HEAD_DIM
    n_blocks = n_pages // 2
    lf_new_t = jnp.pad(jnp.swapaxes(lf_new[:, :, 0:GROUP_HEADS], 1, 2),
                       ((0, 0), (0, 8 - GROUP_HEADS), (0, blk_keys - dec_seq)))
    idx = jnp.arange(blk_keys)
    u = (idx[:, None] > idx[None, :]).astype(BF16)
    subln3 = subln[:, 0:HEAD_DIM].reshape(1, 1, HEAD_DIM)

    def page_idx(which):
        return lambda bi, j, pt: (layer, pt[bi, 2 * (n_blocks - jnp.maximum(j, 1)) + which], 0, 0, 0)

    def lf_idx(which):
        return lambda bi, j, pt: (layer, pt[bi, 2 * (n_blocks - jnp.maximum(j, 1)) + which], 0, 0)

    per_batch4 = lambda bi, j, pt: (bi, 0, 0, 0)
    per_batch3 = lambda bi, j, pt: (bi, 0, 0)
    const2 = lambda bi, j, pt: (0, 0)
    head_block = pl.BlockSpec((None, n_heads, dec_seq, hd), per_batch4)
    new_page = pl.BlockSpec((None, n_heads, hd, page), per_batch4)
    page_shape = (None, None, n_heads, hd, page)
    stat = lambda n: pltpu.VMEM((n, GROUP_HEADS, dec_seq, 1), F32)
    grid_spec = pltpu.PrefetchScalarGridSpec(
        num_scalar_prefetch=1,
        grid=(b, n_blocks + 1),
        in_specs=[
            head_block, new_page, new_page,
            pl.BlockSpec((None, dec_seq, LANES), per_batch3),
            pl.BlockSpec((None, 8, blk_keys), per_batch3),
            pl.BlockSpec(page_shape, page_idx(0)), pl.BlockSpec(page_shape, page_idx(1)),
            pl.BlockSpec(page_shape, page_idx(0)), pl.BlockSpec(page_shape, page_idx(1)),
            pl.BlockSpec((None, None, 8, page), lf_idx(0)), pl.BlockSpec((None, None, 8, page), lf_idx(1)),
            pl.BlockSpec((blk_keys, blk_keys), const2),
            pl.BlockSpec((8, LANES), const2),
            pl.BlockSpec((1, 1, hd), lambda bi, j, pt: (0, 0, 0)),
        ],
        out_specs=head_block,
        scratch_shapes=[
            pltpu.VMEM((N_MAPS, dec_seq, hd), F32),
            stat(3), stat(3), pltpu.VMEM((3, GROUP_HEADS, dec_seq, hd), F32),
            pltpu.VMEM((GROUP_HEADS, dec_seq, hd), F32),
            pltpu.VMEM((GROUP_HEADS, dec_seq, 1), F32),
            pltpu.VMEM((8, 1), F32),
            pltpu.VMEM((GROUP_HEADS, dec_seq, 1), F32),
            stat(n_blocks + 1), stat(n_blocks + 1), stat(n_blocks + 1),
            pltpu.VMEM((n_blocks + 1, GROUP_HEADS, dec_seq, hd), F32),
        ],
    )
    return pl.pallas_call(
        functools.partial(_sample_kernel, n_blocks, page, dec_seq),
        grid_spec=grid_spec,
        out_shape=jax.ShapeDtypeStruct((b, n_heads, dec_seq, hd), F32),
        compiler_params=_cparams(2),
        name="sample_attn",
    )(page_table, q, k_new, v_new, lf_new, lf_new_t, cache_k, cache_k, cache_v, cache_v, cache_lft, cache_lft,
      u, lam_p, subln3)


def kernel(x_prompt, x_sample, cache_k, cache_v, cache_logf, page_table, norm_g, w_in, b_f, w_out,
           diff_lq1, diff_lk1, diff_lq2, diff_lk2, diff_subln_g, ffn_w_gate, ffn_w_up, ffn_w_down):
    depth = norm_g.shape[0]
    bp, seq, d = x_prompt.shape
    bs, dec_seq, _ = x_sample.shape
    assert bp == 1 and d == D_MODEL
    mix = N_HEADS * HEAD_DIM

    def head_major(a):
        return jnp.swapaxes(a.reshape(bs, dec_seq, N_HEADS, HEAD_DIM), 1, 2)

    def new_page(a):
        a = jnp.transpose(a.reshape(bs, dec_seq, N_HEADS, HEAD_DIM), (0, 2, 3, 1))
        return jnp.pad(a, ((0, 0), (0, 0), (0, 0), (0, cache_k.shape[2] - dec_seq)))

    ck = jnp.transpose(cache_k, (0, 1, 3, 4, 2))
    cv = jnp.transpose(cache_v, (0, 1, 3, 4, 2))
    clt = jnp.pad(jnp.swapaxes(cache_logf, 2, 3), ((0, 0), (0, 0), (0, 8 - GROUP_HEADS), (0, 0)))

    moba_qa, moba_ka = alibi_aug(seq, MOBA_SLOPES)
    diff_qa, diff_ka = alibi_aug(seq, DIFF_SLOPES)

    hp = x_prompt.reshape(seq, d)
    hs = x_sample.reshape(bs * dec_seq, d)
    pk, pv, pf, sk, sv, sf = [], [], [], [], [], []
    for l in range(depth):
        lam_init = 0.8 - 0.6 * math.exp(-0.3 * l)
        g = norm_g[l].reshape(6, 1, d)
        wg = ffn_w_gate[l].astype(BF16)
        wu = ffn_w_up[l].astype(BF16)
        wd = ffn_w_down[l].astype(BF16)
        w_qkv = w_in[l, :, 0:3 * mix].astype(BF16)
        w_f = jnp.pad(w_in[l, :, 3 * mix:], ((0, 0), (0, LANES - GROUP_HEADS))).astype(BF16)
        bias_f = jnp.pad(b_f[l].astype(F32), (0, LANES - GROUP_HEADS)).reshape(1, LANES)
        wo = w_out[l].astype(BF16)
        lam_p = _lam_params(diff_lq1[l], diff_lk1[l], diff_lq2[l], diff_lk2[l], lam_init)
        subln = jnp.tile(diff_subln_g[l].astype(F32).reshape(1, HEAD_DIM), (1, 2))

        hp = ffn_block(hp, g[0], g[1], wg[0], wu[0], wd[0])
        q, kf, vf, kb, vt, lf, kmean = project(hp, g[2], w_qkv, w_f, bias_f, True)
        fox_qa, fox_ka = fox_aug(*cumsum_logf(lf))
        n_blk = seq // MOBA_BLOCK
        km = jnp.pad(kmean.reshape(n_blk, GROUP_WIDTH), ((0, LANES - n_blk), (0, 0))).astype(BF16)
        outs = [prompt_attention(G_MOBA, q, kb, vt, (moba_qa, moba_ka, km)),
                prompt_attention(G_FOX, q, kb, vt, (fox_qa, fox_ka)),
                prompt_attention(G_SB, q, kb, vt, ()),
                prompt_attention(G_DIFF, q, kb, vt, (diff_qa, diff_ka, lam_p, subln))]
        hp = out_project(hp, outs, wo, g[3])
        hp = ffn_block(hp, g[4], g[5], wg[1], wu[1], wd[1])
        pk.append(kf.reshape(bp, seq, N_HEADS, HEAD_DIM))
        pv.append(vf.reshape(bp, seq, N_HEADS, HEAD_DIM))
        pf.append(lf[:, 0:GROUP_HEADS].reshape(bp, seq, GROUP_HEADS))

        hs = ffn_block(hs, g[0], g[1], wg[0], wu[0], wd[0])
        q, kf, vf, kb, vb, lf = project(hs, g[2], w_qkv, w_f, bias_f, False)
        o = sample_attention(l, head_major(q), new_page(kf), new_page(vf), lf.reshape(bs, dec_seq, LANES),
                             ck, cv, clt, page_table, lam_p, subln)
        o = jnp.swapaxes(o, 1, 2).reshape(bs * dec_seq, d).astype(BF16)
        outs = [o[:, i * GROUP_WIDTH:(i + 1) * GROUP_WIDTH] for i in range(4)]
        hs = out_project(hs, outs, wo, g[3])
        hs = ffn_block(hs, g[4], g[5], wg[1], wu[1], wd[1])
        sk.append(kf.reshape(bs, dec_seq, N_HEADS, HEAD_DIM))
        sv.append(vf.reshape(bs, dec_seq, N_HEADS, HEAD_DIM))
        sf.append(lf[:, 0:GROUP_HEADS].reshape(bs, dec_seq, GROUP_HEADS))

    return (hp.reshape(bp, seq, d), hs.reshape(bs, dec_seq, d),
            jnp.stack(pk), jnp.stack(pv), jnp.stack(pf), jnp.stack(sk), jnp.stack(sv), jnp.stack(sf))
```
